```python
import math
import jax, jax.numpy as jnp
from jax import lax
import numpy as np

D_MODEL = 1024
BATCH = 32
SEQ = 256
DEPTH = 2
DEC_BATCH = 8
DEC_SEQ = 4096
PAST_LEN = 512

GRID_W = 64
N_EVEN = (DEPTH + 1) // 2
N_ODD = DEPTH // 2
EPS = 1e-6
A_WIDTH = D_MODEL // 2
A_HEADS = 4
A_DK = A_WIDTH // A_HEADS
A_DV = A_WIDTH // A_HEADS
CHUNK = 64
B_WIDTH = D_MODEL // 2
S5_GROUP = 16
S5_GROUPS = B_WIDTH // S5_GROUP
S5_P = 64
C_WIDTH = D_MODEL
CONV_W = 3
EVEN_IN = 5 * A_WIDTH + 2 * B_WIDTH
EVEN_SPLITS = (A_WIDTH, 2 * A_WIDTH, 3 * A_WIDTH, 4 * A_WIDTH, 5 * A_WIDTH, 5 * A_WIDTH + B_WIDTH)
ODD_IN = 4 * C_WIDTH

kernel_name = "hybrid_hgrn2_s5_shortconv_diffusion_step"

F32 = jnp.float32


def _rmsnorm(x, g):
    xf = x.astype(F32)
    y = xf * lax.rsqrt(jnp.mean(xf * xf, axis=-1, keepdims=True) + EPS) * g.astype(F32)
    return y.astype(x.dtype)


def _adaln(cvec, w, b):
    m = jnp.dot(jax.nn.silu(cvec.astype(F32)), w.astype(F32)) + b.astype(F32)
    return jnp.split(m, 3, axis=-1)


def _modulate(h, shift, scale):
    return (h.astype(F32) * (1.0 + scale[:, None, :]) + shift[:, None, :]).astype(h.dtype)


def _hgrn2_chunk_scan(q, k, v, logf, s0):
    bsz, L, H, _ = q.shape
    n = L // CHUNK

    def to_chunks(t):
        return t.reshape(bsz, n, CHUNK, H, t.shape[-1]).transpose(1, 0, 3, 2, 4)

    mask = jnp.tril(jnp.ones((CHUNK, CHUNK), dtype=bool))[:, :, None]

    def step(S, inp):
        qc, kc, vc, gc = inp
        b = jnp.cumsum(gc, axis=2)
        diff = b[:, :, :, None, :] - b[:, :, None, :, :]
        decay = jnp.exp(jnp.where(mask, diff, -jnp.inf))
        scores = jnp.einsum('bhtk,bhtsk,bhsk->bhts', qc, decay, kc)
        o = (jnp.einsum('bhts,bhsv->bhtv', scores, vc)
             + jnp.einsum('bhtk,bhkv->bhtv', qc * jnp.exp(b), S))
        b_last = b[:, :, -1:, :]
        S_new = (jnp.exp(b_last[:, :, 0, :])[..., None] * S
                 + jnp.einsum('bhsk,bhsv->bhkv', kc * jnp.exp(b_last - b), vc))
        return S_new, o

    S_fin, o = lax.scan(step, s0, (to_chunks(q), to_chunks(k), to_chunks(v), to_chunks(logf)))
    o = o.transpose(1, 0, 3, 2, 4).reshape(bsz, L, H, v.shape[-1])
    return o, S_fin


def _hgrn2_dir(q, f_pre, v, lb, s0):
    f = lb + (1.0 - lb) * jax.nn.sigmoid(f_pre)
    heads = lambda t: t.reshape(t.shape[0], t.shape[1], A_HEADS, -1)
    return _hgrn2_chunk_scan(heads(q), heads(1.0 - f), heads(v), heads(jnp.log(f)), s0)


def _complex_affine_combine(e1, e2):
    a1r, a1i, b1r, b1i = e1
    a2r, a2i, b2r, b2i = e2
    return (a2r * a1r - a2i * a1i,
            a2r * a1i + a2i * a1r,
            a2r * b1r - a2i * b1i + b2r,
            a2r * b1i + a2i * b1r + b2i)


def _s5_mixer(u, lam_re, lam_im, log_dt, b_re, b_im, c_re, c_im, d_skip, w_glu, b_glu, x0_re, x0_im):
    bsz, L, _ = u.shape
    uf = u.reshape(bsz, L, S5_GROUPS, S5_GROUP)
    dt = jnp.exp(log_dt)[..., None]
    mag = jnp.exp(lam_re * dt)
    lbar_re = mag * jnp.cos(lam_im * dt)
    lbar_im = mag * jnp.sin(lam_im * dt)
    den = lam_re * lam_re + lam_im * lam_im
    coef_re = ((lbar_re - 1.0) * lam_re + lbar_im * lam_im) / den
    coef_im = (lbar_im * lam_re - (lbar_re - 1.0) * lam_im) / den
    bb_re = coef_re[..., None] * b_re - coef_im[..., None] * b_im
    bb_im = coef_re[..., None] * b_im + coef_im[..., None] * b_re

    def one_dir(d, useq):
        lr, li = lbar_re[d], lbar_im[d]
        xr0, xi0 = x0_re[:, d], x0_im[:, d]
        bu_re = jnp.einsum('blgs,gps->blgp', useq, bb_re[d])
        bu_im = jnp.einsum('blgs,gps->blgp', useq, bb_im[d])
        bu_re = bu_re.at[:, 0].add(lr * xr0 - li * xi0)
        bu_im = bu_im.at[:, 0].add(lr * xi0 + li * xr0)
        a_re = jnp.broadcast_to(lr, bu_re.shape)
        a_im = jnp.broadcast_to(li, bu_im.shape)
        _, _, xr, xi = lax.associative_scan(_complex_affine_combine, (a_re, a_im, bu_re, bu_im), axis=1)
        y = jnp.einsum('blgp,gsp->blgs', xr, c_re[d]) - jnp.einsum('blgp,gsp->blgs', xi, c_im[d])
        return y, xr[:, -1], xi[:, -1]

    y_f, xr_f, xi_f = one_dir(0, uf)
    y_b, xr_b, xi_b = one_dir(1, uf[:, ::-1])
    y = y_f + y_b[:, ::-1] + d_skip.reshape(S5_GROUPS, S5_GROUP) * uf
    y = jax.nn.gelu(y.reshape(bsz, L, B_WIDTH))
    y = y * jax.nn.sigmoid(jnp.dot(y, w_glu) + b_glu)
    return y, jnp.stack([xr_f, xr_b], axis=1), jnp.stack([xi_f, xi_b], axis=1)


def _even_mixer(h, w_in, w_out, lb, hgrn_g, lam_re, lam_im, log_dt, b_re, b_im, c_re, c_im,
                d_skip, w_glu, b_glu, s_hgrn, s_re, s_im):
    bsz, L, _ = h.shape
    z = jnp.dot(h, w_in).astype(F32)
    q, f_fw, f_bw, v, g_a, u, g_b = jnp.split(z, EVEN_SPLITS, axis=-1)
    s_hgrn = s_hgrn.astype(F32)
    o_f, s_f = _hgrn2_dir(q, f_fw, v, lb, s_hgrn[:, 0])
    o_b, s_b = _hgrn2_dir(q[:, ::-1], f_bw[:, ::-1], v[:, ::-1], lb, s_hgrn[:, 1])
    o = o_f + o_b[:, ::-1]
    o = o * lax.rsqrt(jnp.mean(o * o, axis=-1, keepdims=True) + EPS) * hgrn_g.astype(F32).reshape(A_HEADS, A_DV)
    o_a = o.reshape(bsz, L, A_WIDTH) * jax.nn.silu(g_a)
    y_b, n_re, n_im = _s5_mixer(u, lam_re.astype(F32), lam_im.astype(F32), log_dt.astype(F32),
                                b_re.astype(F32), b_im.astype(F32), c_re.astype(F32), c_im.astype(F32),
                                d_skip.astype(F32), w_glu.astype(F32), b_glu.astype(F32),
                                s_re.astype(F32), s_im.astype(F32))
    o_b2 = y_b * jax.nn.silu(g_b)
    out = jnp.dot(jnp.concatenate([o_a, o_b2], axis=-1).astype(h.dtype), w_out)
    return out, jnp.stack([s_f, s_b], axis=1), n_re, n_im


def _row_conv(z, w, b, rows):
    bsz, L, ch = z.shape
    zr = z.reshape(bsz * rows, L // rows, ch)
    zp = jnp.pad(zr, ((0, 0), (1, 1), (0, 0)))
    out = w[0] * zp[:, :-2] + w[1] * zp[:, 1:-1] + w[2] * zp[:, 2:] + b
    return out.reshape(bsz, L, ch)


def _odd_mixer(h, w_in, w_out, conv_w, conv_b, rows):
    z = jnp.dot(h, w_in).astype(F32)
    bg, cg, v, g = jnp.split(z, 4, axis=-1)
    y = bg * _row_conv(cg * v, conv_w.astype(F32), conv_b.astype(F32), rows)
    return jnp.dot((y * jax.nn.silu(g)).astype(h.dtype), w_out)


def setup_inputs(seed: int = 0) -> dict:
    key = jax.random.key(seed)
    ks = jax.random.split(key, 32)
    D = D_MODEL
    nrm = lambda k, shape, s: jax.random.normal(k, shape, F32) * s
    G, P, S = S5_GROUPS, S5_P, S5_GROUP
    n_idx = jnp.arange(P, dtype=F32)
    return {
        "x_prompt": nrm(ks[0], (BATCH, SEQ, D), 1.0),
        "x_sample": nrm(ks[1], (DEC_BATCH, DEC_SEQ, D), 1.0),
        "state_hgrn": nrm(ks[2], (DEC_BATCH, N_EVEN, 2, A_HEADS, A_DK, A_DV), 0.5),
        "state_s5_re": nrm(ks[3], (DEC_BATCH, N_EVEN, 2, G, P), 0.1),
        "state_s5_im": nrm(ks[4], (DEC_BATCH, N_EVEN, 2, G, P), 0.1),
        "c": nrm(ks[5], (DEC_BATCH, D), 1.0),
        "c_ctx": nrm(ks[6], (D,), 1.0),
        "norm_g": 1.0 + nrm(ks[7], (DEPTH, D), 0.05),
        "w_mod": nrm(ks[8], (DEPTH, D, 3 * D), 0.5 * D ** -0.5),
        "b_mod": nrm(ks[9], (DEPTH, 3 * D), 0.02),
        "w_in_even": nrm(ks[10], (N_EVEN, D, EVEN_IN), D ** -0.5),
        "w_out_even": nrm(ks[11], (N_EVEN, A_WIDTH + B_WIDTH, D), (A_WIDTH + B_WIDTH) ** -0.5),
        "lb_logits": nrm(ks[12], (N_EVEN + 1, A_WIDTH), 0.5),
        "hgrn_norm_g": 1.0 + nrm(ks[13], (N_EVEN, A_WIDTH), 0.05),
        "s5_lam_re": -0.5 + nrm(ks[14], (N_EVEN, 2, G, P), 0.01),
        "s5_lam_im": jnp.pi * n_idx + nrm(ks[15], (N_EVEN, 2, G, P), 0.01),
        "s5_log_dt": jax.random.uniform(ks[16], (N_EVEN, 2, G), F32, math.log(1e-3), math.log(1e-1)),
        "s5_b_re": nrm(ks[17], (N_EVEN, 2, G, P, S), (2 * S) ** -0.5),
        "s5_b_im": nrm(ks[18], (N_EVEN, 2, G, P, S), (2 * S) ** -0.5),
        "s5_c_re": nrm(ks[19], (N_EVEN, 2, G, S, P), P ** -0.5),
        "s5_c_im": nrm(ks[20], (N_EVEN, 2, G, S, P), P ** -0.5),
        "s5_d": nrm(ks[21], (N_EVEN, B_WIDTH), 1.0),
        "w_glu": nrm(ks[22], (N_EVEN, B_WIDTH, B_WIDTH), B_WIDTH ** -0.5),
        "b_glu": nrm(ks[23], (N_EVEN, B_WIDTH), 0.02),
        "w_in_odd": nrm(ks[24], (N_ODD, D, ODD_IN), D ** -0.5),
        "w_out_odd": nrm(ks[25], (N_ODD, C_WIDTH, D), C_WIDTH ** -0.5),
        "conv_w": nrm(ks[26], (N_ODD, CONV_W, C_WIDTH), CONV_W ** -0.5),
        "conv_b": nrm(ks[27], (N_ODD, C_WIDTH), 0.02),
        "final_norm_g": 1.0 + nrm(ks[28], (D,), 0.05),
    }


def reference(x_prompt, x_sample, state_hgrn, state_s5_re, state_s5_im, c, c_ctx, norm_g, w_mod, b_mod,
              w_in_even, w_out_even, lb_logits, hgrn_norm_g, s5_lam_re, s5_lam_im, s5_log_dt,
              s5_b_re, s5_b_im, s5_c_re, s5_c_im, s5_d, w_glu, b_glu, w_in_odd, w_out_odd,
              conv_w, conv_b, final_norm_g):
    rows = x_sample.shape[1] // GRID_W
    n_ctx = x_prompt.shape[0]
    lb_all = jnp.cumsum(jax.nn.softmax(lb_logits.astype(F32), axis=0), axis=0)
    zero_hgrn = jnp.zeros((n_ctx, 2, A_HEADS, A_DK, A_DV), F32)
    zero_s5 = jnp.zeros((n_ctx, 2, S5_GROUPS, S5_P), F32)
    yp, ys = x_prompt, x_sample
    st_hgrn, st_re, st_im = [], [], []
    for l in range(DEPTH):
        sh_p, sc_p, g_p = _adaln(c_ctx[None, :], w_mod[l], b_mod[l])
        sh_s, sc_s, g_s = _adaln(c, w_mod[l], b_mod[l])
        hp = _modulate(_rmsnorm(yp, norm_g[l]), sh_p, sc_p)
        hs = _modulate(_rmsnorm(ys, norm_g[l]), sh_s, sc_s)
        j = l // 2
        if l % 2 == 0:
            ev = (w_in_even[j], w_out_even[j], lb_all[j], hgrn_norm_g[j], s5_lam_re[j], s5_lam_im[j],
                  s5_log_dt[j], s5_b_re[j], s5_b_im[j], s5_c_re[j], s5_c_im[j], s5_d[j], w_glu[j], b_glu[j])
            op, sh_new, re_new, im_new = _even_mixer(hp, *ev, zero_hgrn, zero_s5, zero_s5)
            os_, _, _, _ = _even_mixer(hs, *ev, state_hgrn[:, j], state_s5_re[:, j], state_s5_im[:, j])
            st_hgrn.append(sh_new.astype(x_prompt.dtype))
            st_re.append(re_new.astype(x_prompt.dtype))
            st_im.append(im_new.astype(x_prompt.dtype))
        else:
            op = _odd_mixer(hp, w_in_odd[j], w_out_odd[j], conv_w[j], conv_b[j], 1)
            os_ = _odd_mixer(hs, w_in_odd[j], w_out_odd[j], conv_w[j], conv_b[j], rows)
        yp = (yp.astype(F32) + g_p[:, None, :] * op.astype(F32)).astype(x_prompt.dtype)
        ys = (ys.astype(F32) + g_s[:, None, :] * os_.astype(F32)).astype(x_sample.dtype)
    y_prompt = _rmsnorm(yp, final_norm_g)
    y_sample = _rmsnorm(ys, final_norm_g)
    new_state_hgrn = jnp.stack(st_hgrn, axis=1)
    new_state_s5_re = jnp.stack(st_re, axis=1)
    new_state_s5_im = jnp.stack(st_im, axis=1)
    return (y_prompt, y_sample, new_state_hgrn, new_state_s5_re, new_state_s5_im)
```

```python
import functools

import jax
import jax.numpy as jnp
from jax import lax
from jax.experimental import pallas as pl
from jax.experimental.pallas import tpu as pltpu

F32 = jnp.float32
BF16 = jnp.bfloat16
EPS = 1e-6

A_HEADS = 4
HEAD = 128
S5_GROUP = 16
SCAN_CHUNK = 64
TILE = 256
S5_TC = 32
S5_ROWS = 8
S5_CW = 512
LANES = 128


def _dot(a, b):
    return jnp.dot(a, b, preferred_element_type=F32)


def _dot_nt(a, b):
    return lax.dot_general(a, b, (((1,), (1,)), ((), ())), preferred_element_type=F32)


def _dot_tn(a, b):
    return lax.dot_general(a, b, (((0,), (0,)), ((), ())), preferred_element_type=F32)


def _rmsnorm(x, g):
    return x * lax.rsqrt(jnp.mean(x * x, axis=-1, keepdims=True) + EPS) * g


def _params_body(lbl_ref, lre_ref, lim_ref, ldt_ref, bre_ref, bim_ref,
                 lb_ref, lbr_ref, lbi_ref, bbr_ref, bbi_ref):
    logits = lbl_ref[...]
    e = jnp.exp(logits - jnp.max(logits, axis=0, keepdims=True))
    sm = e / jnp.sum(e, axis=0, keepdims=True)
    acc = sm[0:1]
    lb_ref[0:1, :] = acc
    for r in range(1, logits.shape[0]):
        acc = acc + sm[r:r + 1]
        lb_ref[r:r + 1, :] = acc
    lam_re = lre_ref[...]
    lam_im = lim_ref[...]
    dt = jnp.exp(ldt_ref[...])
    mag = jnp.exp(lam_re * dt)
    lbar_re = mag * jnp.cos(lam_im * dt)
    lbar_im = mag * jnp.sin(lam_im * dt)
    den = lam_re * lam_re + lam_im * lam_im
    coef_re = ((lbar_re - 1.0) * lam_re + lbar_im * lam_im) / den
    coef_im = (lbar_im * lam_re - (lbar_re - 1.0) * lam_im) / den
    lbr_ref[...] = lbar_re
    lbi_ref[...] = lbar_im
    b_re = bre_ref[...]
    b_im = bim_ref[...]
    bbr_ref[...] = coef_re * b_re - coef_im * b_im
    bbi_ref[...] = coef_re * b_im + coef_im * b_re


def _prepare_params(lb_logits, lam_re, lam_im, log_dt, b_re, b_im):
    n, s = b_re.shape
    return pl.pallas_call(
        _params_body,
        out_shape=(
            jax.ShapeDtypeStruct(lb_logits.shape, F32),
            jax.ShapeDtypeStruct((n, 1), F32),
            jax.ShapeDtypeStruct((n, 1), F32),
            jax.ShapeDtypeStruct((n, s), F32),
            jax.ShapeDtypeStruct((n, s), F32),
        ),
        name="prepare_params",
    )(lb_logits, lam_re, lam_im, log_dt, b_re, b_im)


def _adaln_body(c_ref, w_ref, b_ref, m_ref):
    c = c_ref[...]
    s = (c * jax.nn.sigmoid(c)).astype(BF16)
    m_ref[...] = _dot(s, w_ref[...].astype(BF16)) + b_ref[...]


def _adaln(cvec, w_mod, b_mod):
    depth, d, d3 = w_mod.shape
    rows = cvec.shape[0]
    return pl.pallas_call(
        _adaln_body,
        grid=(depth, d3 // d),
        in_specs=[
            pl.BlockSpec((rows, d), lambda l, j: (0, 0)),
            pl.BlockSpec((None, d, d), lambda l, j: (l, 0, j)),
            pl.BlockSpec((None, 1, d), lambda l, j: (l, 0, j)),
        ],
        out_specs=pl.BlockSpec((None, rows, d), lambda l, j: (l, 0, j)),
        out_shape=jax.ShapeDtypeStruct((depth, rows, d3), F32),
        name="adaln",
    )(cvec, w_mod, b_mod.reshape(depth, 1, d3))


def _inproj_body(x_ref, mod_ref, ng_ref, lb_ref, w_ref, z_ref, *, d, aw):
    x = x_ref[...]
    mod = mod_ref[0]
    h = _rmsnorm(x, ng_ref[...]) * (1.0 + mod[:, d:2 * d]) + mod[:, 0:d]
    hb = h.astype(BF16)
    lb = lb_ref[...]
    for j in range(7):
        zj = _dot(hb, w_ref[:, j * aw:(j + 1) * aw])
        if j in (1, 2):
            zj = jnp.log(lb + (1.0 - lb) * jax.nn.sigmoid(zj))
        elif j in (4, 6):
            zj = zj * jax.nn.sigmoid(zj)
        z_ref[:, j * aw:(j + 1) * aw] = zj


def _inproj(x, mod, mod_row, norm_g, lb, w_in):
    tokens, d = x.shape
    aw = lb.shape[-1]
    cols = w_in.shape[1]
    return pl.pallas_call(
        functools.partial(_inproj_body, d=d, aw=aw),
        grid=(tokens // TILE,),
        in_specs=[
            pl.BlockSpec((TILE, d), lambda i: (i, 0)),
            pl.BlockSpec((1, 1, 3 * d), lambda i: (mod_row(i), 0, 0)),
            pl.BlockSpec((1, d), lambda i: (0, 0)),
            pl.BlockSpec((1, aw), lambda i: (0, 0)),
            pl.BlockSpec((d, cols), lambda i: (0, 0)),
        ],
        out_specs=pl.BlockSpec((TILE, cols), lambda i: (i, 0)),
        out_shape=jax.ShapeDtypeStruct((tokens, cols), F32),
        compiler_params=pltpu.CompilerParams(dimension_semantics=("parallel",)),
        name="inproj_even",
    )(x, mod, norm_g, lb, w_in)


def _hgrn_body(*refs, nt, nch, has_init, want_state):
    refs = list(refs)
    q_ref, g_ref, v_ref = refs[:3]
    pos = 3
    s0_ref = None
    if has_init:
        s0_ref = refs[pos]
        pos += 1
    o_ref = refs[pos]
    pos += 1
    sout_ref = None
    if want_state:
        sout_ref = refs[pos]
        pos += 1
    st_ref = refs[pos]

    d = pl.program_id(0)
    i = pl.program_id(2)

    @pl.when(i == 0)
    def _():
        for h in range(A_HEADS):
            if has_init:
                st_ref[h] = s0_ref[h].T
            else:
                st_ref[h] = jnp.zeros((HEAD, HEAD), F32)

    ch = SCAN_CHUNK
    row = lax.broadcasted_iota(jnp.int32, (ch, ch), 0)
    col = lax.broadcasted_iota(jnp.int32, (ch, ch), 1)
    mask = (row - col) * (1 - 2 * d) >= 0
    cum = jnp.where(mask, 1.0, 0.0).astype(BF16)

    for j in range(nch):
        cj = j + d * (nch - 1 - 2 * j)
        rows = pl.ds(pl.multiple_of(cj * ch, ch), ch)
        for h in range(A_HEADS):
            cols = slice(h * HEAD, (h + 1) * HEAD)
            g = g_ref[rows, cols]
            q = q_ref[rows, cols]
            v = v_ref[rows, cols].astype(BF16)
            k = 1.0 - jnp.exp(g)
            g_hi = g.astype(BF16)
            g_lo = (g - g_hi.astype(F32)).astype(BF16)
            bb = _dot(cum, jnp.concatenate([g_hi, g_lo], axis=1))
            b = bb[:, :HEAD] + bb[:, HEAD:]
            b_end = jnp.sum(g, axis=0, keepdims=True)
            ref = b[ch // 2:ch // 2 + 1, :]
            qt = (q * jnp.exp(b - ref)).astype(BF16)
            kt = (k * jnp.exp(ref - b)).astype(BF16)
            scores = jnp.where(mask, _dot_nt(qt, kt), 0.0).astype(BF16)
            st = st_ref[h]
            o = _dot(scores, v) + _dot_nt((q * jnp.exp(b)).astype(BF16), st.astype(BF16))
            o_ref[rows, cols] = o
            kd = (k * jnp.exp(b_end - b)).astype(BF16)
            st_ref[h] = st * jnp.exp(b_end) + _dot_tn(v, kd)

    if want_state:
        @pl.when(i == nt - 1)
        def _():
            for h in range(A_HEADS):
                sout_ref[h] = st_ref[h].T


def _hgrn(z, nb, seq, s0, want_state):
    aw = A_HEADS * HEAD
    tb = TILE
    nt = seq // tb
    nch = tb // SCAN_CHUNK
    has_init = s0 is not None

    def tix(d, i):
        return i + d * (nt - 1 - 2 * i)

    in_specs = [
        pl.BlockSpec((tb, aw), lambda d, b, i: (b * nt + tix(d, i), 0)),
        pl.BlockSpec((tb, aw), lambda d, b, i: (b * nt + tix(d, i), 1 + d)),
        pl.BlockSpec((tb, aw), lambda d, b, i: (b * nt + tix(d, i), 3)),
    ]
    args = [z, z, z]
    if has_init:
        in_specs.append(pl.BlockSpec((None, None, A_HEADS, HEAD, HEAD), lambda d, b, i: (b, d, 0, 0, 0)))
        args.append(s0)
    out_specs = [pl.BlockSpec((None, tb, aw), lambda d, b, i: (d, b * nt + tix(d, i), 0))]
    out_shape = [jax.ShapeDtypeStruct((2, nb * seq, aw), F32)]
    if want_state:
        out_specs.append(pl.BlockSpec((None, None, A_HEADS, HEAD, HEAD), lambda d, b, i: (b, d, 0, 0, 0)))
        out_shape.append(jax.ShapeDtypeStruct((nb, 2, A_HEADS, HEAD, HEAD), F32))
    return pl.pallas_call(
        functools.partial(_hgrn_body, nt=nt, nch=nch, has_init=has_init, want_state=want_state),
        grid=(2, nb, nt),
        in_specs=in_specs,
        out_specs=out_specs,
        out_shape=out_shape,
        scratch_shapes=[pltpu.VMEM((A_HEADS, HEAD, HEAD), F32)],
        compiler_params=pltpu.CompilerParams(dimension_semantics=("arbitrary", "arbitrary", "arbitrary")),
        name="hgrn_scan",
    )(*args)


def _s5_body(*refs, nt, tc, has_init, want_state):
    refs = list(refs)
    u_ref, wb_ref, wc_ref, lr_ref, li_ref = refs[:5]
    pos = 5
    x0_ref = None
    if has_init:
        x0_ref = refs[pos]
        pos += 1
    y_ref = refs[pos]
    pos += 1
    xout_ref = None
    if want_state:
        xout_ref = refs[pos]
        pos += 1
    lhs_ref, x_ref, xs_ref, ytm_ref = refs[pos:pos + 4]

    d = pl.program_id(0)
    i = pl.program_id(2)
    nrow = S5_ROWS
    half_in = wb_ref.shape[1]
    half_st = x_ref.shape[1] // 2
    nc = half_st // 2

    @pl.when(i == 0)
    def _():
        if has_init:
            xs_ref[...] = x0_ref[...]
        else:
            xs_ref[...] = jnp.zeros(xs_ref.shape, F32)

    lane = lhs_ref.shape[2]
    npiece = lhs_ref.shape[0]
    for k in range(nrow):
        uk = u_ref[k]
        for j in range(npiece):
            lhs_ref[j, pl.ds(k, tc, stride=nrow), :] = uk[:, j * lane:(j + 1) * lane]
    lhs = jnp.concatenate([lhs_ref[j] for j in range(npiece)], axis=1).astype(BF16)
    for h in range(2):
        x_ref[:, h * half_st:(h + 1) * half_st] = _dot(lhs[:, h * half_in:(h + 1) * half_in], wb_ref[h])

    for h in range(2):
        for c in range(nc // S5_CW):
            cr = slice(h * half_st + c * S5_CW, h * half_st + (c + 1) * S5_CW)
            ci = slice(h * half_st + nc + c * S5_CW, h * half_st + nc + (c + 1) * S5_CW)
            cn = slice(h * nc + c * S5_CW, h * nc + (c + 1) * S5_CW)
            lr = jnp.broadcast_to(lr_ref[:, cn], (nrow, S5_CW))
            li = jnp.broadcast_to(li_ref[:, cn], (nrow, S5_CW))

            def step(t, carry, cr=cr, ci=ci, lr=lr, li=li):
                xr, xi = carry
                te = t + d * (tc - 1 - 2 * t)
                rows = pl.ds(pl.multiple_of(te * nrow, nrow), nrow)
                nxr = lr * xr - li * xi + x_ref[rows, cr]
                nxi = lr * xi + li * xr + x_ref[rows, ci]
                x_ref[rows, cr] = nxr
                x_ref[rows, ci] = nxi
                return nxr, nxi

            xr, xi = lax.fori_loop(0, tc, step, (xs_ref[:, cr], xs_ref[:, ci]), unroll=8)
            xs_ref[:, cr] = xr
            xs_ref[:, ci] = xi

    for h in range(2):
        xh = x_ref[:, h * half_st:(h + 1) * half_st].astype(BF16)
        yh = _dot(xh, wc_ref[h])
        for j in range(npiece // 2):
            ytm_ref[h * (npiece // 2) + j] = yh[:, j * lane:(j + 1) * lane]
    for k in range(nrow):
        y_ref[k] = jnp.concatenate(
            [ytm_ref[j, pl.ds(k, tc, stride=nrow), :] for j in range(npiece)], axis=1)

    if want_state:
        @pl.when(i == nt - 1)
        def _():
            xout_ref[...] = xs_ref[...]


def _s5(z3, wb, wc, lr, li, x0, want_state):
    nb, seq, _ = z3.shape
    bw = wb.shape[2] * 2
    nst = wb.shape[3] * 2
    tc = S5_TC
    nt = seq // tc
    nbg = nb // S5_ROWS
    has_init = x0 is not None

    def tix(d, i):
        return i + d * (nt - 1 - 2 * i)

    in_specs = [
        pl.BlockSpec((S5_ROWS, tc, bw), lambda d, g, i: (g, tix(d, i), 5)),
        pl.BlockSpec((None, 2, bw // 2, nst // 2), lambda d, g, i: (d, 0, 0, 0)),
        pl.BlockSpec((None, 2, nst // 2, bw // 2), lambda d, g, i: (d, 0, 0, 0)),
        pl.BlockSpec((None, 1, nst // 2), lambda d, g, i: (d, 0, 0)),
        pl.BlockSpec((None, 1, nst // 2), lambda d, g, i: (d, 0, 0)),
    ]
    args = [z3, wb, wc, lr, li]
    if has_init:
        in_specs.append(pl.BlockSpec((None, S5_ROWS, nst), lambda d, g, i: (d, g, 0)))
        args.append(x0)
    out_specs = [pl.BlockSpec((None, S5_ROWS, tc, bw), lambda d, g, i: (d, g, tix(d, i), 0))]
    out_shape = [jax.ShapeDtypeStruct((2, nb, seq, bw), F32)]
    if want_state:
        out_specs.append(pl.BlockSpec((None, S5_ROWS, nst), lambda d, g, i: (d, g, 0)))
        out_shape.append(jax.ShapeDtypeStruct((2, nb, nst), F32))
    return pl.pallas_call(
        functools.partial(_s5_body, nt=nt, tc=tc, has_init=has_init, want_state=want_state),
        grid=(2, nbg, nt),
        in_specs=in_specs,
        out_specs=out_specs,
        out_shape=out_shape,
        scratch_shapes=[
            pltpu.VMEM((bw // LANES, tc * S5_ROWS, LANES), F32),
            pltpu.VMEM((tc * S5_ROWS, nst), F32),
            pltpu.VMEM((S5_ROWS, nst), F32),
            pltpu.VMEM((bw // LANES, tc * S5_ROWS, LANES), F32),
        ],
        compiler_params=pltpu.CompilerParams(dimension_semantics=("arbitrary", "arbitrary", "arbitrary")),
        name="s5_scan",
    )(*args)


def _tail_body(x_ref, ga_ref, u_ref, gb_ref, of_ref, ob_ref, yf_ref, yb_ref, m0_ref, m1_ref,
               hg_ref, sd_ref, wglu_ref, bglu_ref, wout_ref, ng_ref, win_ref, cw_ref, cb_ref,
               wout2_ref, fg_ref, out_ref, *, d, seg):
    aw = A_HEADS * HEAD
    o = of_ref[...] + ob_ref[...]
    hg = hg_ref[...]
    heads = []
    for h in range(A_HEADS):
        cols = slice(h * HEAD, (h + 1) * HEAD)
        heads.append(_rmsnorm(o[:, cols], hg[:, cols]))
    o_a = jnp.concatenate(heads, axis=1) * ga_ref[...]
    y = yf_ref[...] + yb_ref[...] + sd_ref[...] * u_ref[...]
    y = jax.nn.gelu(y, approximate=True)
    y = y * jax.nn.sigmoid(_dot(y.astype(BF16), wglu_ref[...]) + bglu_ref[...])
    o_b = y * gb_ref[...]
    mixed = _dot(o_a.astype(BF16), wout_ref[0:aw, :]) + _dot(o_b.astype(BF16), wout_ref[aw:, :])
    m0 = m0_ref[0]
    y1 = x_ref[...] + m0[:, 2 * d:3 * d] * mixed

    m1 = m1_ref[0]
    hb = (_rmsnorm(y1, ng_ref[...]) * (1.0 + m1[:, d:2 * d]) + m1[:, 0:d]).astype(BF16)
    n = y1.shape[0]
    cblk = 256
    pos = lax.broadcasted_iota(jnp.int32, (n, cblk), 0) % seg
    first = pos == 0
    last = pos == seg - 1
    acc = jnp.zeros((n, d), F32)
    for c in range(d // cblk):
        cs = slice(c * cblk, (c + 1) * cblk)
        bg = _dot(hb, win_ref[:, c * cblk:(c + 1) * cblk])
        cg = _dot(hb, win_ref[:, d + c * cblk:d + (c + 1) * cblk])
        vv = _dot(hb, win_ref[:, 2 * d + c * cblk:2 * d + (c + 1) * cblk])
        gg = _dot(hb, win_ref[:, 3 * d + c * cblk:3 * d + (c + 1) * cblk])
        cv = cg * vv
        prev = jnp.where(first, 0.0, pltpu.roll(cv, 1, 0))
        nxt = jnp.where(last, 0.0, pltpu.roll(cv, n - 1, 0))
        conv = cw_ref[0:1, cs] * prev + cw_ref[1:2, cs] * cv + cw_ref[2:3, cs] * nxt + cb_ref[:, cs]
        y2 = bg * conv * (gg * jax.nn.sigmoid(gg))
        acc = acc + _dot(y2.astype(BF16), wout2_ref[cs, :])
    y2 = y1 + m1[:, 2 * d:3 * d] * acc
    out_ref[...] = _rmsnorm(y2, fg_ref[...])


def _tail(x, z, o_dirs, y_dirs, mod0, mod1, mod_row, seg, hgrn_g, s5_d, w_glu, b_glu, w_out,
          norm_g, w_in_odd, conv_w, conv_b, w_out_odd, final_g):
    tokens, d = x.shape
    aw = hgrn_g.shape[-1]
    bw = s5_d.shape[-1]

    def const(shape):
        return pl.BlockSpec(shape, lambda i: tuple(0 for _ in shape))

    in_specs = [
        pl.BlockSpec((TILE, d), lambda i: (i, 0)),
        pl.BlockSpec((TILE, aw), lambda i: (i, 4)),
        pl.BlockSpec((TILE, bw), lambda i: (i, 5)),
        pl.BlockSpec((TILE, bw), lambda i: (i, 6)),
        pl.BlockSpec((None, TILE, aw), lambda i: (0, i, 0)),
        pl.BlockSpec((None, TILE, aw), lambda i: (1, i, 0)),
        pl.BlockSpec((None, TILE, bw), lambda i: (0, i, 0)),
        pl.BlockSpec((None, TILE, bw), lambda i: (1, i, 0)),
        pl.BlockSpec((1, 1, 3 * d), lambda i: (mod_row(i), 0, 0)),
        pl.BlockSpec((1, 1, 3 * d), lambda i: (mod_row(i), 0, 0)),
        const((1, aw)), const((1, bw)), const(w_glu.shape), const((1, bw)), const(w_out.shape),
        const((1, d)), const(w_in_odd.shape), const(conv_w.shape), const((1, d)),
        const(w_out_odd.shape), const((1, d)),
    ]
    return pl.pallas_call(
        functools.partial(_tail_body, d=d, seg=seg),
        grid=(tokens // TILE,),
        in_specs=in_specs,
        out_specs=pl.BlockSpec((TILE, d), lambda i: (i, 0)),
        out_shape=jax.ShapeDtypeStruct((tokens, d), F32),
        compiler_params=pltpu.CompilerParams(dimension_semantics=("parallel",)),
        name="tail",
    )(x, z, z, z, o_dirs, o_dirs, y_dirs, y_dirs, mod0, mod1, hgrn_g, s5_d, w_glu, b_glu, w_out,
      norm_g, w_in_odd, conv_w, conv_b, w_out_odd, final_g)


def _block_diag_halves(m, rows_first):
    ndir, g, a, b = m.shape
    gh = g // 2
    eye = jnp.eye(gh, dtype=m.dtype)
    mh = m.reshape(ndir, 2, gh, a, b)
    out = jnp.einsum('dhgab,gk->dhgakb', mh, eye)
    return out.reshape(ndir, 2, gh * a, gh * b)


def kernel(x_prompt, x_sample, state_hgrn, state_s5_re, state_s5_im, c, c_ctx, norm_g, w_mod, b_mod,
           w_in_even, w_out_even, lb_logits, hgrn_norm_g, s5_lam_re, s5_lam_im, s5_log_dt,
           s5_b_re, s5_b_im, s5_c_re, s5_c_im, s5_d, w_glu, b_glu, w_in_odd, w_out_odd,
           conv_w, conv_b, final_norm_g):
    nbp, seqp, d = x_prompt.shape
    nbs, seqs, _ = x_sample.shape
    depth = norm_g.shape[0]
    n_even, ndir, groups, p = s5_lam_re.shape
    assert depth == 2 and n_even == 1 and ndir == 2, "one even layer followed by one odd layer"
    sg = s5_b_re.shape[-1]
    assert sg == S5_GROUP and lb_logits.shape[-1] == A_HEADS * HEAD
    assert seqp == TILE and seqs % TILE == 0 and nbp % S5_ROWS == 0 and nbs == S5_ROWS
    grid_w = 64
    seg_sample = seqs // (seqs // grid_w)

    nstate = ndir * groups * p
    lb_all, lbar_re, lbar_im, bb_re, bb_im = _prepare_params(
        lb_logits,
        s5_lam_re[0].reshape(nstate, 1), s5_lam_im[0].reshape(nstate, 1),
        jnp.broadcast_to(s5_log_dt[0][:, :, None], (ndir, groups, p)).reshape(nstate, 1),
        s5_b_re[0].reshape(nstate, sg), s5_b_im[0].reshape(nstate, sg))
    lb = lb_all[0:1]
    bb_re = bb_re.reshape(ndir, groups, p, sg).transpose(0, 1, 3, 2)
    bb_im = bb_im.reshape(ndir, groups, p, sg).transpose(0, 1, 3, 2)
    wb = jnp.concatenate([_block_diag_halves(bb_re, True), _block_diag_halves(bb_im, True)], axis=-1).astype(BF16)
    c_re = s5_c_re[0].transpose(0, 1, 3, 2)
    c_im = s5_c_im[0].transpose(0, 1, 3, 2)
    wc = jnp.concatenate([_block_diag_halves(c_re, False), _block_diag_halves(-c_im, False)], axis=-2).astype(BF16)
    lr = lbar_re.reshape(ndir, 1, groups * p)
    li = lbar_im.reshape(ndir, 1, groups * p)

    crow = 16
    cvec = jnp.concatenate([c, c_ctx[None, :], jnp.zeros((crow - nbs - 1, d), F32)], axis=0)
    mod = _adaln(cvec, w_mod, b_mod).reshape(depth, crow, 1, 3 * d)
    tiles_per_seq = seqs // TILE
    row_sample = lambda i: i // tiles_per_seq
    row_prompt = lambda i: nbs

    w_in_b = w_in_even[0].astype(BF16)
    w_out_b = w_out_even[0].astype(BF16)
    w_glu_b = w_glu[0].astype(BF16)
    w_in_odd_b = w_in_odd[0].astype(BF16)
    w_out_odd_b = w_out_odd[0].astype(BF16)

    gh = groups // 2

    def pack_state(re, im):
        nb = re.shape[0]
        re = re.transpose(1, 0, 2, 3).reshape(ndir, nb, 2, gh * p)
        im = im.transpose(1, 0, 2, 3).reshape(ndir, nb, 2, gh * p)
        return jnp.stack([re, im], axis=3).reshape(ndir, nb, 4 * gh * p)

    def unpack_state(xs):
        nb = xs.shape[1]
        xs = xs.reshape(ndir, nb, 2, 2, gh, p)
        re = xs[:, :, :, 0].reshape(ndir, nb, groups, p).transpose(1, 0, 2, 3)
        im = xs[:, :, :, 1].reshape(ndir, nb, groups, p).transpose(1, 0, 2, 3)
        return re, im

    outs = []
    states = None
    for x, nb, seq, mod_row, seg, cached in (
            (x_prompt, nbp, seqp, row_prompt, seqp, False),
            (x_sample, nbs, seqs, row_sample, seg_sample, True)):
        xt = x.reshape(nb * seq, d)
        z = _inproj(xt, mod[0], mod_row, norm_g[0:1], lb, w_in_b)
        s0 = state_hgrn[:, 0] if cached else None
        x0 = pack_state(state_s5_re[:, 0], state_s5_im[:, 0]) if cached else None
        hg = _hgrn(z, nb, seq, s0, not cached)
        s5 = _s5(z.reshape(nb, seq, z.shape[-1]), wb, wc, lr, li, x0, not cached)
        if not cached:
            states = (hg[1], s5[1])
        y = _tail(xt, z, hg[0], s5[0].reshape(2, nb * seq, -1), mod[0], mod[1], mod_row, seg,
                  hgrn_norm_g[0:1], s5_d[0:1], w_glu_b, b_glu[0:1], w_out_b, norm_g[1:2], w_in_odd_b,
                  conv_w[0], conv_b[0:1], w_out_odd_b, final_norm_g[None, :])
        outs.append(y.reshape(nb, seq, d))

    st_hgrn, st_s5 = states
    new_re, new_im = unpack_state(st_s5)
    return (outs[0], outs[1], st_hgrn[:, None], new_re[:, None], new_im[:, None])
```

```python
import functools

import jax
import jax.numpy as jnp
from jax import lax
from jax.experimental import pallas as pl
from jax.experimental.pallas import tpu as pltpu

F32 = jnp.float32
BF16 = jnp.bfloat16
EPS = 1e-6

A_HEADS = 4
HEAD = 128
S5_GROUP = 16
SCAN_CHUNK = 64
EXP_LIMIT = 80.0
TILE = 256
S5_TC = 32
S5_ROWS = 8
LANES = 128
SUBLANES = 8


def _dot(a, b):
    return jnp.dot(a, b, preferred_element_type=F32)


def _dot_nt(a, b):
    return lax.dot_general(a, b, (((1,), (1,)), ((), ())), preferred_element_type=F32)


def _dot_tn(a, b):
    return lax.dot_general(a, b, (((0,), (0,)), ((), ())), preferred_element_type=F32)


def _rmsnorm(x, g):
    return x * lax.rsqrt(jnp.mean(x * x, axis=-1, keepdims=True) + EPS) * g


def _params_body(lbl_ref, lre_ref, lim_ref, ldt_ref, bre_ref, bim_ref,
                 lb_ref, lbr_ref, lbi_ref, bbr_ref, bbi_ref):
    logits = lbl_ref[...]
    e = jnp.exp(logits - jnp.max(logits, axis=0, keepdims=True))
    sm = e / jnp.sum(e, axis=0, keepdims=True)
    acc = sm[0:1]
    lb_ref[0:1, :] = acc
    for r in range(1, logits.shape[0]):
        acc = acc + sm[r:r + 1]
        lb_ref[r:r + 1, :] = acc
    lam_re = lre_ref[...]
    lam_im = lim_ref[...]
    dt = jnp.exp(ldt_ref[...])
    mag = jnp.exp(lam_re * dt)
    lbar_re = mag * jnp.cos(lam_im * dt)
    lbar_im = mag * jnp.sin(lam_im * dt)
    den = lam_re * lam_re + lam_im * lam_im
    coef_re = ((lbar_re - 1.0) * lam_re + lbar_im * lam_im) / den
    coef_im = (lbar_im * lam_re - (lbar_re - 1.0) * lam_im) / den
    lbr_ref[...] = lbar_re
    lbi_ref[...] = lbar_im
    b_re = bre_ref[...]
    b_im = bim_ref[...]
    bbr_ref[...] = coef_re * b_re - coef_im * b_im
    bbi_ref[...] = coef_re * b_im + coef_im * b_re


def _prepare_params(lb_logits, lam_re, lam_im, log_dt, b_re, b_im):
    n, s = b_re.shape
    return pl.pallas_call(
        _params_body,
        out_shape=(
            jax.ShapeDtypeStruct(lb_logits.shape, F32),
            jax.ShapeDtypeStruct((n, 1), F32),
            jax.ShapeDtypeStruct((n, 1), F32),
            jax.ShapeDtypeStruct((n, s), F32),
            jax.ShapeDtypeStruct((n, s), F32),
        ),
        name="prepare_params",
    )(lb_logits, lam_re, lam_im, log_dt, b_re, b_im)


def _adaln_body(c_ref, w_ref, b_ref, m_ref):
    c = c_ref[...]
    s = (c * jax.nn.sigmoid(c)).astype(BF16)
    m_ref[...] = _dot(s, w_ref[...].astype(BF16)) + b_ref[...]


def _adaln(cvec, w_mod, b_mod):
    depth, d, d3 = w_mod.shape
    rows = cvec.shape[0]
    return pl.pallas_call(
        _adaln_body,
        grid=(depth, d3 // d),
        in_specs=[
            pl.BlockSpec((rows, d), lambda l, j: (0, 0)),
            pl.BlockSpec((None, d, d), lambda l, j: (l, 0, j)),
            pl.BlockSpec((None, 1, d), lambda l, j: (l, 0, j)),
        ],
        out_specs=pl.BlockSpec((None, rows, d), lambda l, j: (l, 0, j)),
        out_shape=jax.ShapeDtypeStruct((depth, rows, d3), F32),
        name="adaln",
    )(cvec, w_mod, b_mod.reshape(depth, 1, d3))


def _inproj_body(x_ref, mod_ref, ng_ref, lb_ref, w_ref, z_ref, *, d, aw):
    x = x_ref[...]
    mod = mod_ref[0]
    h = _rmsnorm(x, ng_ref[...]) * (1.0 + mod[:, d:2 * d]) + mod[:, 0:d]
    hb = h.astype(BF16)
    lb = lb_ref[...]
    for j in range(7):
        zj = _dot(hb, w_ref[:, j * aw:(j + 1) * aw])
        if j in (1, 2):
            zj = jnp.log(lb + (1.0 - lb) * jax.nn.sigmoid(zj))
        elif j in (4, 6):
            zj = zj * jax.nn.sigmoid(zj)
        z_ref[:, j * aw:(j + 1) * aw] = zj


def _inproj(x, mod, mod_row, norm_g, lb, w_in):
    tokens, d = x.shape
    aw = lb.shape[-1]
    cols = w_in.shape[1]
    return pl.pallas_call(
        functools.partial(_inproj_body, d=d, aw=aw),
        grid=(tokens // TILE,),
        in_specs=[
            pl.BlockSpec((TILE, d), lambda i: (i, 0)),
            pl.BlockSpec((1, 1, 3 * d), lambda i: (mod_row(i), 0, 0)),
            pl.BlockSpec((1, d), lambda i: (0, 0)),
            pl.BlockSpec((1, aw), lambda i: (0, 0)),
            pl.BlockSpec((d, cols), lambda i: (0, 0)),
        ],
        out_specs=pl.BlockSpec((TILE, cols), lambda i: (i, 0)),
        out_shape=jax.ShapeDtypeStruct((tokens, cols), F32),
        compiler_params=pltpu.CompilerParams(dimension_semantics=("parallel",)),
        name="inproj_even",
    )(x, mod, norm_g, lb, w_in)


def _hgrn_body(*refs, nt, nch, rev, has_init, want_state):
    refs = list(refs)
    q_ref, g_ref, v_ref = refs[:3]
    pos = 3
    s0_ref = None
    if has_init:
        s0_ref = refs[pos]
        pos += 1
    o_ref = refs[pos]
    pos += 1
    sout_ref = None
    if want_state:
        sout_ref = refs[pos]
        pos += 1
    st_ref, b_sc, k_sc, oi_sc = refs[pos:pos + 4]

    i = pl.program_id(1)

    @pl.when(i == 0)
    def _():
        for h in range(A_HEADS):
            if has_init:
                st_ref[h] = s0_ref[h].T
            else:
                st_ref[h] = jnp.zeros((HEAD, HEAD), F32)

    ch = SCAN_CHUNK
    shift = ch.bit_length() - 1
    tb = nch * ch
    aw = A_HEADS * HEAD
    row = lax.broadcasted_iota(jnp.int32, (tb, tb), 0)
    col = lax.broadcasted_iota(jnp.int32, (tb, tb), 1)
    same_chunk = jnp.right_shift(row, shift) == jnp.right_shift(col, shift)
    mask = jnp.logical_and(same_chunk, (col >= row) if rev else (col <= row))
    cum = jnp.where(mask, 1.0, 0.0).astype(BF16)

    g = g_ref[...]
    ends = [jnp.sum(g[c * ch:(c + 1) * ch], axis=0, keepdims=True) for c in range(nch)]
    worst = ends[0]
    for e in ends[1:]:
        worst = jnp.minimum(worst, e)
    safe = jnp.min(worst) > -EXP_LIMIT

    g_hi = g.astype(BF16)
    g_lo = (g - g_hi.astype(F32)).astype(BF16)
    bb = _dot(cum, jnp.concatenate([g_hi, g_lo], axis=1))
    b = bb[:, :aw] + bb[:, aw:]
    mid = ch // 2
    refb = jnp.concatenate(
        [jnp.broadcast_to(b[c * ch + mid:c * ch + mid + 1, :], (ch, aw)) for c in range(nch)], axis=0)
    endb = jnp.concatenate([jnp.broadcast_to(e, (ch, aw)) for e in ends], axis=0)
    k = 1.0 - jnp.exp(g)
    eq = b - refb
    q = q_ref[...]
    qt = (q * jnp.exp(eq)).astype(BF16)
    kt = (k * jnp.exp(-eq)).astype(BF16)
    qb = (q * jnp.exp(b)).astype(BF16)
    kd = (k * jnp.exp(endb - b)).astype(BF16)
    vb = v_ref[...].astype(BF16)

    slot_rows = lax.broadcasted_iota(jnp.int32, (tb, nch * HEAD), 0)
    slot_cols = lax.broadcasted_iota(jnp.int32, (tb, nch * HEAD), 1)
    place = jnp.right_shift(slot_rows, shift) == jnp.right_shift(slot_cols, HEAD.bit_length() - 1)
    order = range(nch - 1, -1, -1) if rev else range(nch)
    for h in range(A_HEADS):
        cols = slice(h * HEAD, (h + 1) * HEAD)
        scores = jnp.where(mask, _dot_nt(qt[:, cols], kt[:, cols]), 0.0).astype(BF16)
        intra = _dot(scores, vb[:, cols])
        kd_slots = jnp.where(place, jnp.concatenate([kd[:, cols]] * nch, axis=1), 0.0)
        incr = _dot_tn(vb[:, cols], kd_slots)
        st = st_ref[h]
        before = [None] * nch
        for c in order:
            before[c] = st.astype(BF16)
            st = st * jnp.exp(ends[c][:, cols]) + incr[:, c * HEAD:(c + 1) * HEAD]
        st_ref[h] = st
        qb_slots = jnp.where(place, jnp.concatenate([qb[:, cols]] * nch, axis=1), 0.0)
        inter = _dot_nt(qb_slots, jnp.concatenate(before, axis=1))
        oi_sc[:, cols] = inter
        o_ref[:, cols] = intra + inter

    @pl.when(jnp.logical_not(safe))
    def _():
        b_sc[...] = b
        k_sc[...] = k
        tpos = lax.broadcasted_iota(jnp.int32, (ch, HEAD), 0)
        for c in range(nch):
            for h in range(A_HEADS):
                r0 = c * ch
                cols = slice(h * HEAD, (h + 1) * HEAD)
                qc = q_ref[r0:r0 + ch, cols]
                bc = b_sc[r0:r0 + ch, cols]

                def body(sg, acc, r0=r0, cols=cols, qc=qc, bc=bc):
                    src = pl.ds(pl.multiple_of(r0 + sg * SUBLANES, SUBLANES), SUBLANES)
                    bs, ks, vs = b_sc[src, cols], k_sc[src, cols], v_ref[src, cols]
                    for j in range(SUBLANES):
                        s = sg * SUBLANES + j
                        valid = (tpos <= s) if rev else (tpos >= s)
                        decay = jnp.exp(jnp.where(valid, bc - bs[j:j + 1], -jnp.inf))
                        a = jnp.sum(qc * decay * ks[j:j + 1], axis=-1, keepdims=True)
                        acc = acc + a * vs[j:j + 1]
                    return acc

                o_ref[r0:r0 + ch, cols] = lax.fori_loop(0, ch // SUBLANES, body, oi_sc[r0:r0 + ch, cols])

    if want_state:
        @pl.when(i == nt - 1)
        def _():
            for h in range(A_HEADS):
                sout_ref[h] = st_ref[h].T


def _hgrn(z, nb, seq, s0, want_state, rev):
    aw = A_HEADS * HEAD
    tb = TILE
    nt = seq // tb
    nch = tb // SCAN_CHUNK
    has_init = s0 is not None
    dirn = int(rev)

    def blk(b, i):
        return b * nt + (nt - 1 - i if rev else i)

    in_specs = [
        pl.BlockSpec((tb, aw), lambda b, i: (blk(b, i), 0)),
        pl.BlockSpec((tb, aw), lambda b, i: (blk(b, i), 1 + dirn)),
        pl.BlockSpec((tb, aw), lambda b, i: (blk(b, i), 3)),
    ]
    args = [z, z, z]
    if has_init:
        in_specs.append(pl.BlockSpec((None, None, A_HEADS, HEAD, HEAD), lambda b, i: (b, dirn, 0, 0, 0)))
        args.append(s0)
    out_specs = [pl.BlockSpec((tb, aw), lambda b, i: (blk(b, i), 0))]
    out_shape = [jax.ShapeDtypeStruct((nb * seq, aw), F32)]
    if want_state:
        out_specs.append(pl.BlockSpec((None, A_HEADS, HEAD, HEAD), lambda b, i: (b, 0, 0, 0)))
        out_shape.append(jax.ShapeDtypeStruct((nb, A_HEADS, HEAD, HEAD), F32))
    return pl.pallas_call(
        functools.partial(_hgrn_body, nt=nt, nch=nch, rev=rev, has_init=has_init, want_state=want_state),
        grid=(nb, nt),
        in_specs=in_specs,
        out_specs=out_specs,
        out_shape=out_shape,
        scratch_shapes=[
            pltpu.VMEM((A_HEADS, HEAD, HEAD), F32),
            pltpu.VMEM((tb, aw), F32),
            pltpu.VMEM((tb, aw), F32),
            pltpu.VMEM((tb, aw), F32),
        ],
        compiler_params=pltpu.CompilerParams(dimension_semantics=("arbitrary", "arbitrary")),
        name="hgrn_bwd" if rev else "hgrn_fwd",
    )(*args)


def _s5_body(*refs, nt, tc, rev, has_init, want_state):
    refs = list(refs)
    u_ref, wb_ref, wc_ref, lr_ref, li_ref = refs[:5]
    pos = 5
    x0_ref = None
    if has_init:
        x0_ref = refs[pos]
        pos += 1
    y_ref = refs[pos]
    pos += 1
    xout_ref = None
    if want_state:
        xout_ref = refs[pos]
        pos += 1
    lhs_ref, x_ref, xs_ref, ytm_ref = refs[pos:pos + 4]

    i = pl.program_id(1)
    nrow = S5_ROWS
    half_in = wb_ref.shape[1]
    half_st = x_ref.shape[1] // 2
    nc = half_st // 2

    @pl.when(i == 0)
    def _():
        if has_init:
            xs_ref[...] = x0_ref[...]
        else:
            xs_ref[...] = jnp.zeros(xs_ref.shape, F32)

    lane = lhs_ref.shape[2]
    npiece = lhs_ref.shape[0]
    for k in range(nrow):
        uk = u_ref[k]
        for j in range(npiece):
            lhs_ref[j, pl.ds(k, tc, stride=nrow), :] = uk[:, j * lane:(j + 1) * lane]
    lhs = jnp.concatenate([lhs_ref[j] for j in range(npiece)], axis=1).astype(BF16)
    for h in range(2):
        x_ref[:, h * half_st:(h + 1) * half_st] = _dot(lhs[:, h * half_in:(h + 1) * half_in], wb_ref[h])

    steps = range(tc - 1, -1, -1) if rev else range(tc)
    for h in range(2):
        cr = slice(h * half_st, h * half_st + nc)
        ci = slice(h * half_st + nc, (h + 1) * half_st)
        lr = jnp.broadcast_to(lr_ref[:, h * nc:(h + 1) * nc], (nrow, nc))
        li = jnp.broadcast_to(li_ref[:, h * nc:(h + 1) * nc], (nrow, nc))
        xr = xs_ref[:, cr]
        xi = xs_ref[:, ci]
        for t in steps:
            rows = slice(t * nrow, (t + 1) * nrow)
            nxr = lr * xr - li * xi + x_ref[rows, cr]
            nxi = lr * xi + li * xr + x_ref[rows, ci]
            x_ref[rows, cr] = nxr
            x_ref[rows, ci] = nxi
            xr, xi = nxr, nxi
        xs_ref[:, cr] = xr
        xs_ref[:, ci] = xi

    for h in range(2):
        xh = x_ref[:, h * half_st:(h + 1) * half_st].astype(BF16)
        yh = _dot(xh, wc_ref[h])
        for j in range(npiece // 2):
            ytm_ref[h * (npiece // 2) + j] = yh[:, j * lane:(j + 1) * lane]
    for k in range(nrow):
        y_ref[k] = jnp.concatenate(
            [ytm_ref[j, pl.ds(k, tc, stride=nrow), :] for j in range(npiece)], axis=1)

    if want_state:
        @pl.when(i == nt - 1)
        def _():
            xout_ref[...] = xs_ref[...]


def _s5(z3, wb, wc, lr, li, x0, want_state, rev):
    nb, seq, _ = z3.shape
    bw = wb.shape[2] * 2
    nst = wb.shape[3] * 2
    tc = S5_TC
    nt = seq // tc
    nbg = nb // S5_ROWS
    has_init = x0 is not None
    dirn = int(rev)

    def tix(i):
        return nt - 1 - i if rev else i

    in_specs = [
        pl.BlockSpec((S5_ROWS, tc, bw), lambda g, i: (g, tix(i), 5)),
        pl.BlockSpec((None, 2, bw // 2, nst // 2), lambda g, i: (dirn, 0, 0, 0)),
        pl.BlockSpec((None, 2, nst // 2, bw // 2), lambda g, i: (dirn, 0, 0, 0)),
        pl.BlockSpec((None, 1, nst // 2), lambda g, i: (dirn, 0, 0)),
        pl.BlockSpec((None, 1, nst // 2), lambda g, i: (dirn, 0, 0)),
    ]
    args = [z3, wb, wc, lr, li]
    if has_init:
        in_specs.append(pl.BlockSpec((None, S5_ROWS, nst), lambda g, i: (dirn, g, 0)))
        args.append(x0)
    out_specs = [pl.BlockSpec((S5_ROWS, tc, bw), lambda g, i: (g, tix(i), 0))]
    out_shape = [jax.ShapeDtypeStruct((nb, seq, bw), F32)]
    if want_state:
        out_specs.append(pl.BlockSpec((S5_ROWS, nst), lambda g, i: (g, 0)))
        out_shape.append(jax.ShapeDtypeStruct((nb, nst), F32))
    return pl.pallas_call(
        functools.partial(_s5_body, nt=nt, tc=tc, rev=rev, has_init=has_init, want_state=want_state),
        grid=(nbg, nt),
        in_specs=in_specs,
        out_specs=out_specs,
        out_shape=out_shape,
        scratch_shapes=[
            pltpu.VMEM((bw // LANES, tc * S5_ROWS, LANES), F32),
            pltpu.VMEM((tc * S5_ROWS, nst), F32),
            pltpu.VMEM((S5_ROWS, nst), F32),
            pltpu.VMEM((bw // LANES, tc * S5_ROWS, LANES), F32),
        ],
        compiler_params=pltpu.CompilerParams(dimension_semantics=("arbitrary", "arbitrary")),
        name="s5_bwd" if rev else "s5_fwd",
    )(*args)


def _tail_body(x_ref, ga_ref, u_ref, gb_ref, of_ref, ob_ref, yf_ref, yb_ref, m0_ref, m1_ref,
               hg_ref, sd_ref, wglu_ref, bglu_ref, wout_ref, ng_ref, win_ref, cw_ref, cb_ref,
               wout2_ref, fg_ref, out_ref, *, d, seg):
    aw = A_HEADS * HEAD
    o = of_ref[...] + ob_ref[...]
    hg = hg_ref[...]
    heads = []
    for h in range(A_HEADS):
        cols = slice(h * HEAD, (h + 1) * HEAD)
        heads.append(_rmsnorm(o[:, cols], hg[:, cols]))
    o_a = jnp.concatenate(heads, axis=1) * ga_ref[...]
    y = yf_ref[...] + yb_ref[...] + sd_ref[...] * u_ref[...]
    y = jax.nn.gelu(y, approximate=True)
    y = y * jax.nn.sigmoid(_dot(y.astype(BF16), wglu_ref[...]) + bglu_ref[...])
    o_b = y * gb_ref[...]
    mixed = _dot(o_a.astype(BF16), wout_ref[0:aw, :]) + _dot(o_b.astype(BF16), wout_ref[aw:, :])
    m0 = m0_ref[0]
    y1 = x_ref[...] + m0[:, 2 * d:3 * d] * mixed

    m1 = m1_ref[0]
    hb = (_rmsnorm(y1, ng_ref[...]) * (1.0 + m1[:, d:2 * d]) + m1[:, 0:d]).astype(BF16)
    n = y1.shape[0]
    cblk = 256
    pos = lax.broadcasted_iota(jnp.int32, (n, cblk), 0) % seg
    first = pos == 0
    last = pos == seg - 1
    acc = jnp.zeros((n, d), F32)
    for c in range(d // cblk):
        cs = slice(c * cblk, (c + 1) * cblk)
        bg = _dot(hb, win_ref[:, c * cblk:(c + 1) * cblk])
        cg = _dot(hb, win_ref[:, d + c * cblk:d + (c + 1) * cblk])
        vv = _dot(hb, win_ref[:, 2 * d + c * cblk:2 * d + (c + 1) * cblk])
        gg = _dot(hb, win_ref[:, 3 * d + c * cblk:3 * d + (c + 1) * cblk])
        cv = cg * vv
        prev = jnp.where(first, 0.0, pltpu.roll(cv, 1, 0))
        nxt = jnp.where(last, 0.0, pltpu.roll(cv, n - 1, 0))
        conv = cw_ref[0:1, cs] * prev + cw_ref[1:2, cs] * cv + cw_ref[2:3, cs] * nxt + cb_ref[:, cs]
        y2 = bg * conv * (gg * jax.nn.sigmoid(gg))
        acc = acc + _dot(y2.astype(BF16), wout2_ref[cs, :])
    y2 = y1 + m1[:, 2 * d:3 * d] * acc
    out_ref[...] = _rmsnorm(y2, fg_ref[...])


def _tail(x, z, o_f, o_b, y_f, y_b, mod0, mod1, mod_row, seg, hgrn_g, s5_d, w_glu, b_glu, w_out,
          norm_g, w_in_odd, conv_w, conv_b, w_out_odd, final_g):
    tokens, d = x.shape
    aw = hgrn_g.shape[-1]
    bw = s5_d.shape[-1]

    def const(shape):
        return pl.BlockSpec(shape, lambda i: tuple(0 for _ in shape))

    in_specs = [
        pl.BlockSpec((TILE, d), lambda i: (i, 0)),
        pl.BlockSpec((TILE, aw), lambda i: (i, 4)),
        pl.BlockSpec((TILE, bw), lambda i: (i, 5)),
        pl.BlockSpec((TILE, bw), lambda i: (i, 6)),
        pl.BlockSpec((TILE, aw), lambda i: (i, 0)),
        pl.BlockSpec((TILE, aw), lambda i: (i, 0)),
        pl.BlockSpec((TILE, bw), lambda i: (i, 0)),
        pl.BlockSpec((TILE, bw), lambda i: (i, 0)),
        pl.BlockSpec((1, 1, 3 * d), lambda i: (mod_row(i), 0, 0)),
        pl.BlockSpec((1, 1, 3 * d), lambda i: (mod_row(i), 0, 0)),
        const((1, aw)), const((1, bw)), const(w_glu.shape), const((1, bw)), const(w_out.shape),
        const((1, d)), const(w_in_odd.shape), const(conv_w.shape), const((1, d)),
        const(w_out_odd.shape), const((1, d)),
    ]
    return pl.pallas_call(
        functools.partial(_tail_body, d=d, seg=seg),
        grid=(tokens // TILE,),
        in_specs=in_specs,
        out_specs=pl.BlockSpec((TILE, d), lambda i: (i, 0)),
        out_shape=jax.ShapeDtypeStruct((tokens, d), F32),
        compiler_params=pltpu.CompilerParams(dimension_semantics=("parallel",)),
        name="tail",
    )(x, z, z, z, o_f, o_b, y_f, y_b, mod0, mod1, hgrn_g, s5_d, w_glu, b_glu, w_out,
      norm_g, w_in_odd, conv_w, conv_b, w_out_odd, final_g)


def _block_diag_halves(m):
    ndir, g, a, b = m.shape
    gh = g // 2
    eye = jnp.eye(gh, dtype=m.dtype)
    mh = m.reshape(ndir, 2, gh, a, b)
    out = jnp.einsum('dhgab,gk->dhgakb', mh, eye)
    return out.reshape(ndir, 2, gh * a, gh * b)


def kernel(x_prompt, x_sample, state_hgrn, state_s5_re, state_s5_im, c, c_ctx, norm_g, w_mod, b_mod,
           w_in_even, w_out_even, lb_logits, hgrn_norm_g, s5_lam_re, s5_lam_im, s5_log_dt,
           s5_b_re, s5_b_im, s5_c_re, s5_c_im, s5_d, w_glu, b_glu, w_in_odd, w_out_odd,
           conv_w, conv_b, final_norm_g):
    nbp, seqp, d = x_prompt.shape
    nbs, seqs, _ = x_sample.shape
    depth = norm_g.shape[0]
    n_even, ndir, groups, p = s5_lam_re.shape
    assert depth == 2 and n_even == 1 and ndir == 2, "one even layer followed by one odd layer"
    sg = s5_b_re.shape[-1]
    assert sg == S5_GROUP and lb_logits.shape[-1] == A_HEADS * HEAD
    assert seqp == TILE and seqs % TILE == 0 and nbp % S5_ROWS == 0 and nbs == S5_ROWS
    grid_w = 64
    seg_sample = seqs // (seqs // grid_w)

    nstate = ndir * groups * p
    lb_all, lbar_re, lbar_im, bb_re, bb_im = _prepare_params(
        lb_logits,
        s5_lam_re[0].reshape(nstate, 1), s5_lam_im[0].reshape(nstate, 1),
        jnp.broadcast_to(s5_log_dt[0][:, :, None], (ndir, groups, p)).reshape(nstate, 1),
        s5_b_re[0].reshape(nstate, sg), s5_b_im[0].reshape(nstate, sg))
    lb = lb_all[0:1]
    bb_re = bb_re.reshape(ndir, groups, p, sg).transpose(0, 1, 3, 2)
    bb_im = bb_im.reshape(ndir, groups, p, sg).transpose(0, 1, 3, 2)
    wb = jnp.concatenate([_block_diag_halves(bb_re), _block_diag_halves(bb_im)], axis=-1).astype(BF16)
    c_re = s5_c_re[0].transpose(0, 1, 3, 2)
    c_im = s5_c_im[0].transpose(0, 1, 3, 2)
    wc = jnp.concatenate([_block_diag_halves(c_re), _block_diag_halves(-c_im)], axis=-2).astype(BF16)
    lr = lbar_re.reshape(ndir, 1, groups * p)
    li = lbar_im.reshape(ndir, 1, groups * p)

    crow = 16
    cvec = jnp.concatenate([c, c_ctx[None, :], jnp.zeros((crow - nbs - 1, d), F32)], axis=0)
    mod = _adaln(cvec, w_mod, b_mod).reshape(depth, crow, 1, 3 * d)
    tiles_per_seq = seqs // TILE
    row_sample = lambda i: i // tiles_per_seq
    row_prompt = lambda i: nbs

    w_in_b = w_in_even[0].astype(BF16)
    w_out_b = w_out_even[0].astype(BF16)
    w_glu_b = w_glu[0].astype(BF16)
    w_in_odd_b = w_in_odd[0].astype(BF16)
    w_out_odd_b = w_out_odd[0].astype(BF16)

    gh = groups // 2

    def pack_state(re, im):
        nb = re.shape[0]
        re = re.transpose(1, 0, 2, 3).reshape(ndir, nb, 2, gh * p)
        im = im.transpose(1, 0, 2, 3).reshape(ndir, nb, 2, gh * p)
        return jnp.stack([re, im], axis=3).reshape(ndir, nb, 4 * gh * p)

    def unpack_state(xs):
        nb = xs.shape[1]
        xs = xs.reshape(ndir, nb, 2, 2, gh, p)
        re = xs[:, :, :, 0].reshape(ndir, nb, groups, p).transpose(1, 0, 2, 3)
        im = xs[:, :, :, 1].reshape(ndir, nb, groups, p).transpose(1, 0, 2, 3)
        return re, im

    outs = []
    states = None
    for x, nb, seq, mod_row, seg, cached in (
            (x_prompt, nbp, seqp, row_prompt, seqp, False),
            (x_sample, nbs, seqs, row_sample, seg_sample, True)):
        xt = x.reshape(nb * seq, d)
        z = _inproj(xt, mod[0], mod_row, norm_g[0:1], lb, w_in_b)
        z3 = z.reshape(nb, seq, z.shape[-1])
        s0 = state_hgrn[:, 0] if cached else None
        x0 = pack_state(state_s5_re[:, 0], state_s5_im[:, 0]) if cached else None
        hg = [_hgrn(z, nb, seq, s0, not cached, rev) for rev in (False, True)]
        s5 = [_s5(z3, wb, wc, lr, li, x0, not cached, rev) for rev in (False, True)]
        if not cached:
            states = (jnp.stack([hg[0][1], hg[1][1]], axis=1), jnp.stack([s5[0][1], s5[1][1]], axis=0))
        y = _tail(xt, z, hg[0][0], hg[1][0], s5[0][0].reshape(nb * seq, -1), s5[1][0].reshape(nb * seq, -1),
                  mod[0], mod[1], mod_row, seg, hgrn_norm_g[0:1], s5_d[0:1], w_glu_b, b_glu[0:1], w_out_b,
                  norm_g[1:2], w_in_odd_b, conv_w[0], conv_b[0:1], w_out_odd_b, final_norm_g[None, :])
        outs.append(y.reshape(nb, seq, d))

    st_hgrn, st_s5 = states
    new_re, new_im = unpack_state(st_s5)
    return (outs[0], outs[1], st_hgrn[:, None], new_re[:, None], new_im[:, None])
```

```python
import functools

import jax
import jax.numpy as jnp
from jax import lax
from jax.experimental import pallas as pl
from jax.experimental.pallas import tpu as pltpu

F32 = jnp.float32
BF16 = jnp.bfloat16
EPS = 1e-6

A_HEADS = 4
HEAD = 128
S5_GROUP = 16
SCAN_CHUNK = 64
EXP_LIMIT = 80.0
TILE = 512
SUB_TILE = 256
HGRN_BLOCK = 256
S5_TC = 64
S5_ROWS = 8
LANES = 128
SUBLANES = 8


def _dot(a, b):
    return jnp.dot(a, b, preferred_element_type=F32)


def _dot_nt(a, b):
    return lax.dot_general(a, b, (((1,), (1,)), ((), ())), preferred_element_type=F32)


def _dot_tn(a, b):
    return lax.dot_general(a, b, (((0,), (0,)), ((), ())), preferred_element_type=F32)


def _rmsnorm(x, g):
    return x * lax.rsqrt(jnp.mean(x * x, axis=-1, keepdims=True) + EPS) * g


def _params_body(lbl_ref, lre_ref, lim_ref, ldt_ref, bre_ref, bim_ref,
                 lb_ref, lbr_ref, lbi_ref, bbr_ref, bbi_ref):
    logits = lbl_ref[...]
    e = jnp.exp(logits - jnp.max(logits, axis=0, keepdims=True))
    sm = e / jnp.sum(e, axis=0, keepdims=True)
    acc = sm[0:1]
    lb_ref[0:1, :] = acc
    for r in range(1, logits.shape[0]):
        acc = acc + sm[r:r + 1]
        lb_ref[r:r + 1, :] = acc
    lam_re = lre_ref[...]
    lam_im = lim_ref[...]
    dt = jnp.exp(ldt_ref[...])
    mag = jnp.exp(lam_re * dt)
    lbar_re = mag * jnp.cos(lam_im * dt)
    lbar_im = mag * jnp.sin(lam_im * dt)
    den = lam_re * lam_re + lam_im * lam_im
    coef_re = ((lbar_re - 1.0) * lam_re + lbar_im * lam_im) / den
    coef_im = (lbar_im * lam_re - (lbar_re - 1.0) * lam_im) / den
    lbr_ref[...] = lbar_re
    lbi_ref[...] = lbar_im
    b_re = bre_ref[...]
    b_im = bim_ref[...]
    bbr_ref[...] = coef_re * b_re - coef_im * b_im
    bbi_ref[...] = coef_re * b_im + coef_im * b_re


def _prepare_params(lb_logits, lam_re, lam_im, log_dt, b_re, b_im):
    n, s = b_re.shape
    return pl.pallas_call(
        _params_body,
        out_shape=(
            jax.ShapeDtypeStruct(lb_logits.shape, F32),
            jax.ShapeDtypeStruct((n, 1), F32),
            jax.ShapeDtypeStruct((n, 1), F32),
            jax.ShapeDtypeStruct((n, s), F32),
            jax.ShapeDtypeStruct((n, s), F32),
        ),
        name="prepare_params",
    )(lb_logits, lam_re, lam_im, log_dt, b_re, b_im)


def _adaln_body(c_ref, w_ref, b_ref, m_ref):
    c = c_ref[...]
    s = (c * jax.nn.sigmoid(c)).astype(BF16)
    m_ref[...] = _dot(s, w_ref[...].astype(BF16)) + b_ref[...]


def _adaln(cvec, w_mod, b_mod):
    depth, d, d3 = w_mod.shape
    rows = cvec.shape[0]
    return pl.pallas_call(
        _adaln_body,
        grid=(depth, d3 // d),
        in_specs=[
            pl.BlockSpec((rows, d), lambda l, j: (0, 0)),
            pl.BlockSpec((None, d, d), lambda l, j: (l, 0, j)),
            pl.BlockSpec((None, 1, d), lambda l, j: (l, 0, j)),
        ],
        out_specs=pl.BlockSpec((None, rows, d), lambda l, j: (l, 0, j)),
        out_shape=jax.ShapeDtypeStruct((depth, rows, d3), F32),
        name="adaln",
    )(cvec, w_mod, b_mod.reshape(depth, 1, d3))


def _inproj_body(x_ref, mod_ref, ng_ref, lb_ref, w_ref, z_ref, *, d, aw):
    mod = mod_ref[0]
    lb = lb_ref[...]
    for r in range(x_ref.shape[0] // SUB_TILE):
        rows = slice(r * SUB_TILE, (r + 1) * SUB_TILE)
        h = _rmsnorm(x_ref[rows, :], ng_ref[...]) * (1.0 + mod[:, d:2 * d]) + mod[:, 0:d]
        hb = h.astype(BF16)
        for j in range(7):
            zj = _dot(hb, w_ref[:, j * aw:(j + 1) * aw])
            if j in (1, 2):
                zj = jnp.log(lb + (1.0 - lb) * jax.nn.sigmoid(zj))
            elif j in (4, 6):
                zj = zj * jax.nn.sigmoid(zj)
            z_ref[rows, j * aw:(j + 1) * aw] = zj


def _inproj(x, mod, mod_row, norm_g, lb, w_in):
    tokens, d = x.shape
    aw = lb.shape[-1]
    cols = w_in.shape[1]
    return pl.pallas_call(
        functools.partial(_inproj_body, d=d, aw=aw),
        grid=(tokens // TILE,),
        in_specs=[
            pl.BlockSpec((TILE, d), lambda i: (i, 0)),
            pl.BlockSpec((1, 1, 3 * d), lambda i: (mod_row(i), 0, 0)),
            pl.BlockSpec((1, d), lambda i: (0, 0)),
            pl.BlockSpec((1, aw), lambda i: (0, 0)),
            pl.BlockSpec((d, cols), lambda i: (0, 0), pipeline_mode=pl.Buffered(1)),
        ],
        out_specs=pl.BlockSpec((TILE, cols), lambda i: (i, 0)),
        out_shape=jax.ShapeDtypeStruct((tokens, cols), F32),
        compiler_params=pltpu.CompilerParams(dimension_semantics=("parallel",)),
        name="inproj_even",
    )(x, mod, norm_g, lb, w_in)


def _hgrn_body(*refs, nt, nch, rev, has_init, want_state):
    refs = list(refs)
    q_ref, g_ref, v_ref = refs[:3]
    pos = 3
    s0_ref = None
    if has_init:
        s0_ref = refs[pos]
        pos += 1
    o_ref = refs[pos]
    pos += 1
    sout_ref = None
    if want_state:
        sout_ref = refs[pos]
        pos += 1
    st_ref, b_sc, k_sc, oi_sc = refs[pos:pos + 4]

    i = pl.program_id(1)

    @pl.when(i == 0)
    def _():
        for h in range(A_HEADS):
            if has_init:
                st_ref[h] = s0_ref[h].T
            else:
                st_ref[h] = jnp.zeros((HEAD, HEAD), F32)

    ch = SCAN_CHUNK
    shift = ch.bit_length() - 1
    tb = nch * ch
    aw = A_HEADS * HEAD
    row = lax.broadcasted_iota(jnp.int32, (tb, tb), 0)
    col = lax.broadcasted_iota(jnp.int32, (tb, tb), 1)
    same_chunk = jnp.right_shift(row, shift) == jnp.right_shift(col, shift)
    mask = jnp.logical_and(same_chunk, (col >= row) if rev else (col <= row))
    cum = jnp.where(mask, 1.0, 0.0).astype(BF16)

    g = g_ref[...]
    ends = [jnp.sum(g[c * ch:(c + 1) * ch], axis=0, keepdims=True) for c in range(nch)]
    worst = ends[0]
    for e in ends[1:]:
        worst = jnp.minimum(worst, e)
    safe = jnp.min(worst) > -EXP_LIMIT

    g_hi = g.astype(BF16)
    g_lo = (g - g_hi.astype(F32)).astype(BF16)
    bb = _dot(cum, jnp.concatenate([g_hi, g_lo], axis=1))
    b = bb[:, :aw] + bb[:, aw:]
    mid = ch // 2
    refb = jnp.concatenate(
        [jnp.broadcast_to(b[c * ch + mid:c * ch + mid + 1, :], (ch, aw)) for c in range(nch)], axis=0)
    endb = jnp.concatenate([jnp.broadcast_to(e, (ch, aw)) for e in ends], axis=0)
    k = 1.0 - jnp.exp(g)
    eq = b - refb
    q = q_ref[...]
    qt = (q * jnp.exp(eq)).astype(BF16)
    kt = (k * jnp.exp(-eq)).astype(BF16)
    qb = (q * jnp.exp(b)).astype(BF16)
    kd = (k * jnp.exp(endb - b)).astype(BF16)
    vb = v_ref[...].astype(BF16)

    slot_rows = lax.broadcasted_iota(jnp.int32, (tb, nch * HEAD), 0)
    slot_cols = lax.broadcasted_iota(jnp.int32, (tb, nch * HEAD), 1)
    place = jnp.right_shift(slot_rows, shift) == jnp.right_shift(slot_cols, HEAD.bit_length() - 1)
    order = range(nch - 1, -1, -1) if rev else range(nch)
    for h in range(A_HEADS):
        cols = slice(h * HEAD, (h + 1) * HEAD)
        scores = jnp.where(mask, _dot_nt(qt[:, cols], kt[:, cols]), 0.0).astype(BF16)
        intra = _dot(scores, vb[:, cols])
        kd_slots = jnp.where(place, jnp.concatenate([kd[:, cols]] * nch, axis=1), 0.0)
        incr = _dot_tn(vb[:, cols], kd_slots)
        st = st_ref[h]
        before = [None] * nch
        for c in order:
            before[c] = st.astype(BF16)
            st = st * jnp.exp(ends[c][:, cols]) + incr[:, c * HEAD:(c + 1) * HEAD]
        st_ref[h] = st
        qb_slots = jnp.where(place, jnp.concatenate([qb[:, cols]] * nch, axis=1), 0.0)
        inter = _dot_nt(qb_slots, jnp.concatenate(before, axis=1))
        oi_sc[:, cols] = inter
        o_ref[:, cols] = intra + inter

    @pl.when(jnp.logical_not(safe))
    def _():
        b_sc[...] = b
        k_sc[...] = k
        tpos = lax.broadcasted_iota(jnp.int32, (ch, HEAD), 0)
        for c in range(nch):
            for h in range(A_HEADS):
                r0 = c * ch
                cols = slice(h * HEAD, (h + 1) * HEAD)
                qc = q_ref[r0:r0 + ch, cols]
                bc = b_sc[r0:r0 + ch, cols]

                def body(sg, acc, r0=r0, cols=cols, qc=qc, bc=bc):
                    src = pl.ds(pl.multiple_of(r0 + sg * SUBLANES, SUBLANES), SUBLANES)
                    bs, ks, vs = b_sc[src, cols], k_sc[src, cols], v_ref[src, cols]
                    for j in range(SUBLANES):
                        s = sg * SUBLANES + j
                        valid = (tpos <= s) if rev else (tpos >= s)
                        decay = jnp.exp(jnp.where(valid, bc - bs[j:j + 1], -jnp.inf))
                        a = jnp.sum(qc * decay * ks[j:j + 1], axis=-1, keepdims=True)
                        acc = acc + a * vs[j:j + 1]
                    return acc

                o_ref[r0:r0 + ch, cols] = lax.fori_loop(0, ch // SUBLANES, body, oi_sc[r0:r0 + ch, cols])

    if want_state:
        @pl.when(i == nt - 1)
        def _():
            for h in range(A_HEADS):
                sout_ref[h] = st_ref[h].T


def _hgrn(z, nb, seq, s0, want_state, rev):
    aw = A_HEADS * HEAD
    tb = HGRN_BLOCK
    nt = seq // tb
    nch = tb // SCAN_CHUNK
    has_init = s0 is not None
    dirn = int(rev)

    def blk(b, i):
        return b * nt + (nt - 1 - i if rev else i)

    in_specs = [
        pl.BlockSpec((tb, aw), lambda b, i: (blk(b, i), 0)),
        pl.BlockSpec((tb, aw), lambda b, i: (blk(b, i), 1 + dirn)),
        pl.BlockSpec((tb, aw), lambda b, i: (blk(b, i), 3)),
    ]
    args = [z, z, z]
    if has_init:
        in_specs.append(pl.BlockSpec((None, None, A_HEADS, HEAD, HEAD), lambda b, i: (b, dirn, 0, 0, 0)))
        args.append(s0)
    out_specs = [pl.BlockSpec((tb, aw), lambda b, i: (blk(b, i), 0))]
    out_shape = [jax.ShapeDtypeStruct((nb * seq, aw), F32)]
    if want_state:
        out_specs.append(pl.BlockSpec((None, A_HEADS, HEAD, HEAD), lambda b, i: (b, 0, 0, 0)))
        out_shape.append(jax.ShapeDtypeStruct((nb, A_HEADS, HEAD, HEAD), F32))
    return pl.pallas_call(
        functools.partial(_hgrn_body, nt=nt, nch=nch, rev=rev, has_init=has_init, want_state=want_state),
        grid=(nb, nt),
        in_specs=in_specs,
        out_specs=out_specs,
        out_shape=out_shape,
        scratch_shapes=[
            pltpu.VMEM((A_HEADS, HEAD, HEAD), F32),
            pltpu.VMEM((tb, aw), F32),
            pltpu.VMEM((tb, aw), F32),
            pltpu.VMEM((tb, aw), F32),
        ],
        compiler_params=pltpu.CompilerParams(dimension_semantics=("arbitrary", "arbitrary")),
        name="hgrn_bwd" if rev else "hgrn_fwd",
    )(*args)


def _s5_body(*refs, nt, tc, rev, has_init, want_state):
    refs = list(refs)
    u_ref, wb_ref, wc_ref, lr_ref, li_ref = refs[:5]
    pos = 5
    x0_ref = None
    if has_init:
        x0_ref = refs[pos]
        pos += 1
    y_ref = refs[pos]
    pos += 1
    xout_ref = None
    if want_state:
        xout_ref = refs[pos]
        pos += 1
    lhs_ref, x_ref, xs_ref, ytm_ref = refs[pos:pos + 4]

    i = pl.program_id(1)
    nrow = S5_ROWS
    half_in = wb_ref.shape[1]
    half_st = x_ref.shape[1] // 2
    nc = half_st // 2

    @pl.when(i == 0)
    def _():
        if has_init:
            xs_ref[...] = x0_ref[...]
        else:
            xs_ref[...] = jnp.zeros(xs_ref.shape, F32)

    lane = lhs_ref.shape[2]
    npiece = lhs_ref.shape[0]
    for k in range(nrow):
        uk = u_ref[k]
        for j in range(npiece):
            lhs_ref[j, pl.ds(k, tc, stride=nrow), :] = uk[:, j * lane:(j + 1) * lane]
    lhs = jnp.concatenate([lhs_ref[j] for j in range(npiece)], axis=1).astype(BF16)
    for h in range(2):
        x_ref[:, h * half_st:(h + 1) * half_st] = _dot(lhs[:, h * half_in:(h + 1) * half_in], wb_ref[h])

    steps = range(tc - 1, -1, -1) if rev else range(tc)
    for h in range(2):
        cr = slice(h * half_st, h * half_st + nc)
        ci = slice(h * half_st + nc, (h + 1) * half_st)
        lr = jnp.broadcast_to(lr_ref[:, h * nc:(h + 1) * nc], (nrow, nc))
        li = jnp.broadcast_to(li_ref[:, h * nc:(h + 1) * nc], (nrow, nc))
        xr = xs_ref[:, cr]
        xi = xs_ref[:, ci]
        for t in steps:
            rows = slice(t * nrow, (t + 1) * nrow)
            nxr = lr * xr - li * xi + x_ref[rows, cr]
            nxi = lr * xi + li * xr + x_ref[rows, ci]
            x_ref[rows, cr] = nxr
            x_ref[rows, ci] = nxi
            xr, xi = nxr, nxi
        xs_ref[:, cr] = xr
        xs_ref[:, ci] = xi

    for h in range(2):
        xh = x_ref[:, h * half_st:(h + 1) * half_st].astype(BF16)
        yh = _dot(xh, wc_ref[h])
        for j in range(npiece // 2):
            ytm_ref[h * (npiece // 2) + j] = yh[:, j * lane:(j + 1) * lane]
    for k in range(nrow):
        y_ref[k] = jnp.concatenate(
            [ytm_ref[j, pl.ds(k, tc, stride=nrow), :] for j in range(npiece)], axis=1)

    if want_state:
        @pl.when(i == nt - 1)
        def _():
            xout_ref[...] = xs_ref[...]


def _s5(z3, wb, wc, lr, li, x0, want_state, rev):
    nb, seq, _ = z3.shape
    bw = wb.shape[2] * 2
    nst = wb.shape[3] * 2
    tc = S5_TC
    nt = seq // tc
    nbg = nb // S5_ROWS
    has_init = x0 is not None
    dirn = int(rev)

    def tix(i):
        return nt - 1 - i if rev else i

    in_specs = [
        pl.BlockSpec((S5_ROWS, tc, bw), lambda g, i: (g, tix(i), 5)),
        pl.BlockSpec((None, 2, bw // 2, nst // 2), lambda g, i: (dirn, 0, 0, 0)),
        pl.BlockSpec((None, 2, nst // 2, bw // 2), lambda g, i: (dirn, 0, 0, 0)),
        pl.BlockSpec((None, 1, nst // 2), lambda g, i: (dirn, 0, 0)),
        pl.BlockSpec((None, 1, nst // 2), lambda g, i: (dirn, 0, 0)),
    ]
    args = [z3, wb, wc, lr, li]
    if has_init:
        in_specs.append(pl.BlockSpec((None, S5_ROWS, nst), lambda g, i: (dirn, g, 0)))
        args.append(x0)
    out_specs = [pl.BlockSpec((S5_ROWS, tc, bw), lambda g, i: (g, tix(i), 0))]
    out_shape = [jax.ShapeDtypeStruct((nb, seq, bw), F32)]
    if want_state:
        out_specs.append(pl.BlockSpec((S5_ROWS, nst), lambda g, i: (g, 0)))
        out_shape.append(jax.ShapeDtypeStruct((nb, nst), F32))
    return pl.pallas_call(
        functools.partial(_s5_body, nt=nt, tc=tc, rev=rev, has_init=has_init, want_state=want_state),
        grid=(nbg, nt),
        in_specs=in_specs,
        out_specs=out_specs,
        out_shape=out_shape,
        scratch_shapes=[
            pltpu.VMEM((bw // LANES, tc * S5_ROWS, LANES), F32),
            pltpu.VMEM((tc * S5_ROWS, nst), F32),
            pltpu.VMEM((S5_ROWS, nst), F32),
            pltpu.VMEM((bw // LANES, tc * S5_ROWS, LANES), F32),
        ],
        compiler_params=pltpu.CompilerParams(dimension_semantics=("arbitrary", "arbitrary")),
        name="s5_bwd" if rev else "s5_fwd",
    )(*args)


def _tail_rows(rows, x_ref, ga_ref, u_ref, gb_ref, of_ref, ob_ref, yf_ref, yb_ref, m0, m1,
               hg_ref, sd_ref, wglu_ref, bglu_ref, wout_ref, ng_ref, win_ref, cw_ref, cb_ref,
               wout2_ref, fg_ref, out_ref, *, d, seg):
    aw = A_HEADS * HEAD
    o = of_ref[rows, :] + ob_ref[rows, :]
    hg = hg_ref[...]
    heads = []
    for h in range(A_HEADS):
        cols = slice(h * HEAD, (h + 1) * HEAD)
        heads.append(_rmsnorm(o[:, cols], hg[:, cols]))
    o_a = jnp.concatenate(heads, axis=1) * ga_ref[rows, :]
    y = yf_ref[rows, :] + yb_ref[rows, :] + sd_ref[...] * u_ref[rows, :]
    y = jax.nn.gelu(y, approximate=True)
    y = y * jax.nn.sigmoid(_dot(y.astype(BF16), wglu_ref[...]) + bglu_ref[...])
    o_b = y * gb_ref[rows, :]
    mixed = _dot(o_a.astype(BF16), wout_ref[0:aw, :]) + _dot(o_b.astype(BF16), wout_ref[aw:, :])
    y1 = x_ref[rows, :] + m0[:, 2 * d:3 * d] * mixed

    hb = (_rmsnorm(y1, ng_ref[...]) * (1.0 + m1[:, d:2 * d]) + m1[:, 0:d]).astype(BF16)
    n = y1.shape[0]
    cblk = 256
    pos = lax.broadcasted_iota(jnp.int32, (n, cblk), 0) % seg
    first = pos == 0
    last = pos == seg - 1
    acc = jnp.zeros((n, d), F32)
    for c in range(d // cblk):
        cs = slice(c * cblk, (c + 1) * cblk)
        bg = _dot(hb, win_ref[:, c * cblk:(c + 1) * cblk])
        cg = _dot(hb, win_ref[:, d + c * cblk:d + (c + 1) * cblk])
        vv = _dot(hb, win_ref[:, 2 * d + c * cblk:2 * d + (c + 1) * cblk])
        gg = _dot(hb, win_ref[:, 3 * d + c * cblk:3 * d + (c + 1) * cblk])
        cv = cg * vv
        prev = jnp.where(first, 0.0, pltpu.roll(cv, 1, 0))
        nxt = jnp.where(last, 0.0, pltpu.roll(cv, n - 1, 0))
        conv = cw_ref[0:1, cs] * prev + cw_ref[1:2, cs] * cv + cw_ref[2:3, cs] * nxt + cb_ref[:, cs]
        y2 = bg * conv * (gg * jax.nn.sigmoid(gg))
        acc = acc + _dot(y2.astype(BF16), wout2_ref[cs, :])
    y2 = y1 + m1[:, 2 * d:3 * d] * acc
    out_ref[rows, :] = _rmsnorm(y2, fg_ref[...])


def _tail_body(x_ref, ga_ref, u_ref, gb_ref, of_ref, ob_ref, yf_ref, yb_ref, m0_ref, m1_ref, *rest, d, seg):
    m0 = m0_ref[0]
    m1 = m1_ref[0]
    for r in range(x_ref.shape[0] // SUB_TILE):
        rows = slice(r * SUB_TILE, (r + 1) * SUB_TILE)
        _tail_rows(rows, x_ref, ga_ref, u_ref, gb_ref, of_ref, ob_ref, yf_ref, yb_ref, m0, m1, *rest, d=d, seg=seg)


def _tail(x, z, o_f, o_b, y_f, y_b, mod0, mod1, mod_row, seg, hgrn_g, s5_d, w_glu, b_glu, w_out,
          norm_g, w_in_odd, conv_w, conv_b, w_out_odd, final_g):
    tokens, d = x.shape
    aw = hgrn_g.shape[-1]
    bw = s5_d.shape[-1]

    def const(shape):
        return pl.BlockSpec(shape, lambda i: tuple(0 for _ in shape), pipeline_mode=pl.Buffered(1))

    in_specs = [
        pl.BlockSpec((TILE, d), lambda i: (i, 0)),
        pl.BlockSpec((TILE, aw), lambda i: (i, 4)),
        pl.BlockSpec((TILE, bw), lambda i: (i, 5)),
        pl.BlockSpec((TILE, bw), lambda i: (i, 6)),
        pl.BlockSpec((TILE, aw), lambda i: (i, 0)),
        pl.BlockSpec((TILE, aw), lambda i: (i, 0)),
        pl.BlockSpec((TILE, bw), lambda i: (i, 0)),
        pl.BlockSpec((TILE, bw), lambda i: (i, 0)),
        pl.BlockSpec((1, 1, 3 * d), lambda i: (mod_row(i), 0, 0)),
        pl.BlockSpec((1, 1, 3 * d), lambda i: (mod_row(i), 0, 0)),
        const((1, aw)), const((1, bw)), const(w_glu.shape), const((1, bw)), const(w_out.shape),
        const((1, d)), const(w_in_odd.shape), const(conv_w.shape), const((1, d)),
        const(w_out_odd.shape), const((1, d)),
    ]
    return pl.pallas_call(
        functools.partial(_tail_body, d=d, seg=seg),
        grid=(tokens // TILE,),
        in_specs=in_specs,
        out_specs=pl.BlockSpec((TILE, d), lambda i: (i, 0)),
        out_shape=jax.ShapeDtypeStruct((tokens, d), F32),
        compiler_params=pltpu.CompilerParams(dimension_semantics=("parallel",)),
        name="tail",
    )(x, z, z, z, o_f, o_b, y_f, y_b, mod0, mod1, hgrn_g, s5_d, w_glu, b_glu, w_out,
      norm_g, w_in_odd, conv_w, conv_b, w_out_odd, final_g)


def _block_diag_halves(m):
    ndir, g, a, b = m.shape
    gh = g // 2
    eye = jnp.eye(gh, dtype=m.dtype)
    mh = m.reshape(ndir, 2, gh, a, b)
    out = jnp.einsum('dhgab,gk->dhgakb', mh, eye)
    return out.reshape(ndir, 2, gh * a, gh * b)


def kernel(x_prompt, x_sample, state_hgrn, state_s5_re, state_s5_im, c, c_ctx, norm_g, w_mod, b_mod,
           w_in_even, w_out_even, lb_logits, hgrn_norm_g, s5_lam_re, s5_lam_im, s5_log_dt,
           s5_b_re, s5_b_im, s5_c_re, s5_c_im, s5_d, w_glu, b_glu, w_in_odd, w_out_odd,
           conv_w, conv_b, final_norm_g):
    nbp, seqp, d = x_prompt.shape
    nbs, seqs, _ = x_sample.shape
    depth = norm_g.shape[0]
    n_even, ndir, groups, p = s5_lam_re.shape
    assert depth == 2 and n_even == 1 and ndir == 2, "one even layer followed by one odd layer"
    sg = s5_b_re.shape[-1]
    assert sg == S5_GROUP and lb_logits.shape[-1] == A_HEADS * HEAD
    assert seqp == HGRN_BLOCK == SUB_TILE and seqs % TILE == 0 and (nbp * seqp) % TILE == 0
    assert nbp % S5_ROWS == 0 and nbs == S5_ROWS
    grid_w = 64
    seg_sample = seqs // (seqs // grid_w)

    nstate = ndir * groups * p
    lb_all, lbar_re, lbar_im, bb_re, bb_im = _prepare_params(
        lb_logits,
        s5_lam_re[0].reshape(nstate, 1), s5_lam_im[0].reshape(nstate, 1),
        jnp.broadcast_to(s5_log_dt[0][:, :, None], (ndir, groups, p)).reshape(nstate, 1),
        s5_b_re[0].reshape(nstate, sg), s5_b_im[0].reshape(nstate, sg))
    lb = lb_all[0:1]
    bb_re = bb_re.reshape(ndir, groups, p, sg).transpose(0, 1, 3, 2)
    bb_im = bb_im.reshape(ndir, groups, p, sg).transpose(0, 1, 3, 2)
    wb = jnp.concatenate([_block_diag_halves(bb_re), _block_diag_halves(bb_im)], axis=-1).astype(BF16)
    c_re = s5_c_re[0].transpose(0, 1, 3, 2)
    c_im = s5_c_im[0].transpose(0, 1, 3, 2)
    wc = jnp.concatenate([_block_diag_halves(c_re), _block_diag_halves(-c_im)], axis=-2).astype(BF16)
    lr = lbar_re.reshape(ndir, 1, groups * p)
    li = lbar_im.reshape(ndir, 1, groups * p)

    crow = 16
    cvec = jnp.concatenate([c, c_ctx[None, :], jnp.zeros((crow - nbs - 1, d), F32)], axis=0)
    mod = _adaln(cvec, w_mod, b_mod).reshape(depth, crow, 1, 3 * d)
    tiles_per_seq = seqs // TILE
    row_sample = lambda i: i // tiles_per_seq
    row_prompt = lambda i: nbs

    w_in_b = w_in_even[0].astype(BF16)
    w_out_b = w_out_even[0].astype(BF16)
    w_glu_b = w_glu[0].astype(BF16)
    w_in_odd_b = w_in_odd[0].astype(BF16)
    w_out_odd_b = w_out_odd[0].astype(BF16)

    gh = groups // 2

    def pack_state(re, im):
        nb = re.shape[0]
        re = re.transpose(1, 0, 2, 3).reshape(ndir, nb, 2, gh * p)
        im = im.transpose(1, 0, 2, 3).reshape(ndir, nb, 2, gh * p)
        return jnp.stack([re, im], axis=3).reshape(ndir, nb, 4 * gh * p)

    def unpack_state(xs):
        nb = xs.shape[1]
        xs = xs.reshape(ndir, nb, 2, 2, gh, p)
        re = xs[:, :, :, 0].reshape(ndir, nb, groups, p).transpose(1, 0, 2, 3)
        im = xs[:, :, :, 1].reshape(ndir, nb, groups, p).transpose(1, 0, 2, 3)
        return re, im

    outs = []
    states = None
    for x, nb, seq, mod_row, seg, cached in (
            (x_prompt, nbp, seqp, row_prompt, seqp, False),
            (x_sample, nbs, seqs, row_sample, seg_sample, True)):
        xt = x.reshape(nb * seq, d)
        z = _inproj(xt, mod[0], mod_row, norm_g[0:1], lb, w_in_b)
        z3 = z.reshape(nb, seq, z.shape[-1])
        s0 = state_hgrn[:, 0] if cached else None
        x0 = pack_state(state_s5_re[:, 0], state_s5_im[:, 0]) if cached else None
        hg = [_hgrn(z, nb, seq, s0, not cached, rev) for rev in (False, True)]
        s5 = [_s5(z3, wb, wc, lr, li, x0, not cached, rev) for rev in (False, True)]
        if not cached:
            states = (jnp.stack([hg[0][1], hg[1][1]], axis=1), jnp.stack([s5[0][1], s5[1][1]], axis=0))
        y = _tail(xt, z, hg[0][0], hg[1][0], s5[0][0].reshape(nb * seq, -1), s5[1][0].reshape(nb * seq, -1),
                  mod[0], mod[1], mod_row, seg, hgrn_norm_g[0:1], s5_d[0:1], w_glu_b, b_glu[0:1], w_out_b,
                  norm_g[1:2], w_in_odd_b, conv_w[0], conv_b[0:1], w_out_odd_b, final_norm_g[None, :])
        outs.append(y.reshape(nb, seq, d))

    st_hgrn, st_s5 = states
    new_re, new_im = unpack_state(st_s5)
    return (outs[0], outs[1], st_hgrn[:, None], new_re[:, None], new_im[:, None])
```

```python
import functools

import jax
import jax.numpy as jnp
from jax import lax
from jax.experimental import pallas as pl
from jax.experimental.pallas import tpu as pltpu

F32 = jnp.float32
BF16 = jnp.bfloat16
EPS = 1e-6

A_HEADS = 4
HEAD = 128
S5_GROUP = 16
SCAN_CHUNK = 64
EXP_LIMIT = 80.0
TILE = 512
SUB_TILE = 256
HGRN_BLOCK = 512
HGRN_SUB = 256
N_IN_BLOCKS = 7
S5_TC = 64
S5_ROWS = 8
LANES = 128
SUBLANES = 8


def _dot(a, b):
    return jnp.dot(a, b, preferred_element_type=F32)


def _dot_nt(a, b):
    return lax.dot_general(a, b, (((1,), (1,)), ((), ())), preferred_element_type=F32)


def _dot_tn(a, b):
    return lax.dot_general(a, b, (((0,), (0,)), ((), ())), preferred_element_type=F32)


def _rmsnorm(x, g):
    return x * lax.rsqrt(jnp.mean(x * x, axis=-1, keepdims=True) + EPS) * g


def _params_body(lbl_ref, lre_ref, lim_ref, ldt_ref, bre_ref, bim_ref,
                 lb_ref, lbr_ref, lbi_ref, bbr_ref, bbi_ref):
    logits = lbl_ref[...]
    e = jnp.exp(logits - jnp.max(logits, axis=0, keepdims=True))
    sm = e / jnp.sum(e, axis=0, keepdims=True)
    acc = sm[0:1]
    lb_ref[0:1, :] = acc
    for r in range(1, logits.shape[0]):
        acc = acc + sm[r:r + 1]
        lb_ref[r:r + 1, :] = acc
    lam_re = lre_ref[...]
    lam_im = lim_ref[...]
    dt = jnp.exp(ldt_ref[...])
    mag = jnp.exp(lam_re * dt)
    lbar_re = mag * jnp.cos(lam_im * dt)
    lbar_im = mag * jnp.sin(lam_im * dt)
    den = lam_re * lam_re + lam_im * lam_im
    coef_re = ((lbar_re - 1.0) * lam_re + lbar_im * lam_im) / den
    coef_im = (lbar_im * lam_re - (lbar_re - 1.0) * lam_im) / den
    lbr_ref[...] = lbar_re
    lbi_ref[...] = lbar_im
    b_re = bre_ref[...]
    b_im = bim_ref[...]
    bbr_ref[...] = coef_re * b_re - coef_im * b_im
    bbi_ref[...] = coef_re * b_im + coef_im * b_re


def _prepare_params(lb_logits, lam_re, lam_im, log_dt, b_re, b_im):
    n, s = b_re.shape
    return pl.pallas_call(
        _params_body,
        out_shape=(
            jax.ShapeDtypeStruct(lb_logits.shape, F32),
            jax.ShapeDtypeStruct((n, 1), F32),
            jax.ShapeDtypeStruct((n, 1), F32),
            jax.ShapeDtypeStruct((n, s), F32),
            jax.ShapeDtypeStruct((n, s), F32),
        ),
        name="prepare_params",
    )(lb_logits, lam_re, lam_im, log_dt, b_re, b_im)


def _adaln_body(c_ref, w_ref, b_ref, m_ref):
    c = c_ref[...]
    s = (c * jax.nn.sigmoid(c)).astype(BF16)
    m_ref[...] = _dot(s, w_ref[...].astype(BF16)) + b_ref[...]


def _adaln(cvec, w_mod, b_mod):
    depth, d, d3 = w_mod.shape
    rows = cvec.shape[0]
    return pl.pallas_call(
        _adaln_body,
        grid=(depth, d3 // d),
        in_specs=[
            pl.BlockSpec((rows, d), lambda l, j: (0, 0)),
            pl.BlockSpec((None, d, d), lambda l, j: (l, 0, j)),
            pl.BlockSpec((None, 1, d), lambda l, j: (l, 0, j)),
        ],
        out_specs=pl.BlockSpec((None, rows, d), lambda l, j: (l, 0, j)),
        out_shape=jax.ShapeDtypeStruct((depth, rows, d3), F32),
        name="adaln",
    )(cvec, w_mod, b_mod.reshape(depth, 1, d3))


def _inproj_body(x_ref, mod_ref, ng_ref, lb_ref, w_ref, z_ref, *, d, aw):
    mod = mod_ref[0]
    lb = lb_ref[...]
    for r in range(x_ref.shape[0] // SUB_TILE):
        rows = slice(r * SUB_TILE, (r + 1) * SUB_TILE)
        h = _rmsnorm(x_ref[rows, :], ng_ref[...]) * (1.0 + mod[:, d:2 * d]) + mod[:, 0:d]
        hb = h.astype(BF16)
        for j in range(N_IN_BLOCKS):
            zj = _dot(hb, w_ref[:, j * aw:(j + 1) * aw])
            if j in (1, 2):
                sg = jax.nn.sigmoid(zj)
                z_ref[rows, (N_IN_BLOCKS - 1 + j) * aw:(N_IN_BLOCKS + j) * aw] = (1.0 - lb) * (1.0 - sg)
                zj = jnp.log(lb + (1.0 - lb) * sg)
            elif j in (4, 6):
                zj = zj * jax.nn.sigmoid(zj)
            z_ref[rows, j * aw:(j + 1) * aw] = zj


def _inproj(x, mod, mod_row, norm_g, lb, w_in):
    tokens, d = x.shape
    aw = lb.shape[-1]
    cols = w_in.shape[1]
    assert cols == N_IN_BLOCKS * aw
    zcols = cols + 2 * aw
    return pl.pallas_call(
        functools.partial(_inproj_body, d=d, aw=aw),
        grid=(tokens // TILE,),
        in_specs=[
            pl.BlockSpec((TILE, d), lambda i: (i, 0)),
            pl.BlockSpec((1, 1, 3 * d), lambda i: (mod_row(i), 0, 0)),
            pl.BlockSpec((1, d), lambda i: (0, 0)),
            pl.BlockSpec((1, aw), lambda i: (0, 0)),
            pl.BlockSpec((d, cols), lambda i: (0, 0), pipeline_mode=pl.Buffered(1)),
        ],
        out_specs=pl.BlockSpec((TILE, zcols), lambda i: (i, 0)),
        out_shape=jax.ShapeDtypeStruct((tokens, zcols), F32),
        compiler_params=pltpu.CompilerParams(dimension_semantics=("parallel",)),
        name="inproj_even",
    )(x, mod, norm_g, lb, w_in)


def _hgrn_body(*refs, nt, nch, rev, has_init, want_state):
    refs = list(refs)
    q_ref, g_ref, k_ref, v_ref = refs[:4]
    pos = 4
    s0_ref = None
    if has_init:
        s0_ref = refs[pos]
        pos += 1
    o_ref = refs[pos]
    pos += 1
    sout_ref = None
    if want_state:
        sout_ref = refs[pos]
        pos += 1
    st_ref, sold_ref, b_sc = refs[pos:pos + 3]

    i = pl.program_id(1)

    @pl.when(i == 0)
    def _():
        for h in range(A_HEADS):
            if has_init:
                st_ref[h] = s0_ref[h].T
            else:
                st_ref[h] = jnp.zeros((HEAD, HEAD), F32)

    ch = SCAN_CHUNK
    shift = ch.bit_length() - 1
    sub = nch * ch
    nsub = q_ref.shape[0] // sub
    aw = A_HEADS * HEAD
    row = lax.broadcasted_iota(jnp.int32, (sub, sub), 0)
    col = lax.broadcasted_iota(jnp.int32, (sub, sub), 1)
    same_chunk = jnp.right_shift(row, shift) == jnp.right_shift(col, shift)
    mask = jnp.logical_and(same_chunk, (col >= row) if rev else (col <= row))
    cum = jnp.where(mask, 1.0, 0.0).astype(BF16)
    mid = ch // 2
    sub_order = range(nsub - 1, -1, -1) if rev else range(nsub)
    order = range(nch - 1, -1, -1) if rev else range(nch)
    heads = [slice(h * HEAD, (h + 1) * HEAD) for h in range(A_HEADS)]
    zero = jnp.zeros((ch, HEAD), BF16)

    def slots(x, h):
        rows = []
        for c in range(nch):
            piece = x[c * ch:(c + 1) * ch, h * HEAD:(h + 1) * HEAD]
            rows.append(jnp.concatenate([zero] * c + [piece] + [zero] * (nch - 1 - c), axis=1))
        return jnp.concatenate(rows, axis=0)

    ends, factors, worst = [], [], None
    for s in range(nsub):
        rs = slice(s * sub, (s + 1) * sub)
        g = g_ref[rs, :]
        e = [jnp.sum(g[c * ch:(c + 1) * ch], axis=0, keepdims=True) for c in range(nch)]
        ends.append(e)
        for x in e:
            worst = x if worst is None else jnp.minimum(worst, x)
        g_hi = g.astype(BF16)
        g_lo = (g - g_hi.astype(F32)).astype(BF16)
        bb = _dot(cum, jnp.concatenate([g_hi, g_lo], axis=1))
        b = bb[:, :aw] + bb[:, aw:]
        b_sc[rs, :] = b
        q = q_ref[rs, :]
        k = k_ref[rs, :]
        qt, kt, qb, kd = [], [], [], []
        for c in range(nch):
            r = slice(c * ch, (c + 1) * ch)
            ref = b[c * ch + mid:c * ch + mid + 1, :]
            grow = jnp.exp(b[r] - ref)
            qg = q[r] * grow
            kg = k[r] / grow
            qt.append(qg.astype(BF16))
            kt.append(kg.astype(BF16))
            qb.append((qg * jnp.exp(ref)).astype(BF16))
            kd.append((kg * jnp.exp(e[c] - ref)).astype(BF16))
        factors.append(tuple(jnp.concatenate(parts, axis=0) for parts in (qt, kt, qb, kd))
                       + (v_ref[rs, :].astype(BF16),))
    safe = jnp.min(worst) > -EXP_LIMIT

    raw = [[_dot_nt(f[0][:, cols], f[1][:, cols]) for cols in heads] for f in factors]
    incr = [[_dot_tn(f[4][:, cols], slots(f[3], h)) for h, cols in enumerate(heads)] for f in factors]
    scores = [[jnp.where(mask, x, 0.0).astype(BF16) for x in per_sub] for per_sub in raw]
    before = [[None] * A_HEADS for _ in range(nsub)]
    for h, cols in enumerate(heads):
        st = st_ref[h]
        sold_ref[h] = st
        for s in sub_order:
            bef = [None] * nch
            for c in order:
                bef[c] = st.astype(BF16)
                st = st * jnp.exp(ends[s][c][:, cols]) + incr[s][h][:, c * HEAD:(c + 1) * HEAD]
            before[s][h] = jnp.concatenate(bef, axis=1)
        st_ref[h] = st
    for s, f in enumerate(factors):
        for h, cols in enumerate(heads):
            o_ref[s * sub:(s + 1) * sub, cols] = (
                _dot(scores[s][h], f[4][:, cols]) + _dot_nt(slots(f[2], h), before[s][h]))

    @pl.when(jnp.logical_not(safe))
    def _():
        tpos = lax.broadcasted_iota(jnp.int32, (ch, HEAD), 0)
        for h, cols in enumerate(heads):
            st = sold_ref[h]
            for s in sub_order:
                for c in order:
                    r0 = s * sub + c * ch
                    r = slice(r0, r0 + ch)
                    qc = q_ref[r, cols]
                    bc = b_sc[r, cols]
                    end = ends[s][c][:, cols]

                    def body(sg, acc, r0=r0, cols=cols, qc=qc, bc=bc):
                        src = pl.ds(pl.multiple_of(r0 + sg * SUBLANES, SUBLANES), SUBLANES)
                        bs, ks, vs = b_sc[src, cols], k_ref[src, cols], v_ref[src, cols]
                        for j in range(SUBLANES):
                            t = sg * SUBLANES + j
                            valid = (tpos <= t) if rev else (tpos >= t)
                            decay = jnp.exp(jnp.where(valid, bc - bs[j:j + 1], -jnp.inf))
                            a = jnp.sum(qc * decay * ks[j:j + 1], axis=-1, keepdims=True)
                            acc = acc + a * vs[j:j + 1]
                        return acc

                    inter = _dot_nt((qc * jnp.exp(bc)).astype(BF16), st.astype(BF16))
                    o_ref[r, cols] = lax.fori_loop(0, ch // SUBLANES, body, inter)
                    kdc = (k_ref[r, cols] * jnp.exp(end - bc)).astype(BF16)
                    st = st * jnp.exp(end) + _dot_tn(v_ref[r, cols].astype(BF16), kdc)
            st_ref[h] = st

    if want_state:
        @pl.when(i == nt - 1)
        def _():
            for h in range(A_HEADS):
                sout_ref[h] = st_ref[h].T


def _hgrn(z, nb, seq, s0, want_state, rev):
    aw = A_HEADS * HEAD
    tb = min(HGRN_BLOCK, seq)
    nt = seq // tb
    nch = HGRN_SUB // SCAN_CHUNK
    has_init = s0 is not None
    dirn = int(rev)

    def blk(b, i):
        return b * nt + (nt - 1 - i if rev else i)

    in_specs = [
        pl.BlockSpec((tb, aw), lambda b, i: (blk(b, i), 0)),
        pl.BlockSpec((tb, aw), lambda b, i: (blk(b, i), 1 + dirn)),
        pl.BlockSpec((tb, aw), lambda b, i: (blk(b, i), N_IN_BLOCKS + dirn)),
        pl.BlockSpec((tb, aw), lambda b, i: (blk(b, i), 3)),
    ]
    args = [z, z, z, z]
    if has_init:
        in_specs.append(pl.BlockSpec((None, None, A_HEADS, HEAD, HEAD), lambda b, i: (b, dirn, 0, 0, 0)))
        args.append(s0)
    out_specs = [pl.BlockSpec((tb, aw), lambda b, i: (blk(b, i), 0))]
    out_shape = [jax.ShapeDtypeStruct((nb * seq, aw), F32)]
    if want_state:
        out_specs.append(pl.BlockSpec((None, A_HEADS, HEAD, HEAD), lambda b, i: (b, 0, 0, 0)))
        out_shape.append(jax.ShapeDtypeStruct((nb, A_HEADS, HEAD, HEAD), F32))
    return pl.pallas_call(
        functools.partial(_hgrn_body, nt=nt, nch=nch, rev=rev, has_init=has_init, want_state=want_state),
        grid=(nb, nt),
        in_specs=in_specs,
        out_specs=out_specs,
        out_shape=out_shape,
        scratch_shapes=[
            pltpu.VMEM((A_HEADS, HEAD, HEAD), F32),
            pltpu.VMEM((A_HEADS, HEAD, HEAD), F32),
            pltpu.VMEM((tb, aw), F32),
        ],
        compiler_params=pltpu.CompilerParams(dimension_semantics=("arbitrary", "arbitrary")),
        name="hgrn_bwd" if rev else "hgrn_fwd",
    )(*args)


def _s5_body(*refs, nt, tc, rev, has_init, want_state):
    refs = list(refs)
    u_ref, wb_ref, wc_ref, lr_ref, li_ref = refs[:5]
    pos = 5
    x0_ref = None
    if has_init:
        x0_ref = refs[pos]
        pos += 1
    y_ref = refs[pos]
    pos += 1
    xout_ref = None
    if want_state:
        xout_ref = refs[pos]
        pos += 1
    lhs_ref, x_ref, xs_ref, ytm_ref = refs[pos:pos + 4]

    i = pl.program_id(1)
    nrow = S5_ROWS
    half_in = wb_ref.shape[1]
    half_st = x_ref.shape[1] // 2
    nc = half_st // 2

    @pl.when(i == 0)
    def _():
        if has_init:
            xs_ref[...] = x0_ref[...]
        else:
            xs_ref[...] = jnp.zeros(xs_ref.shape, F32)

    lane = lhs_ref.shape[2]
    npiece = lhs_ref.shape[0]
    for k in range(nrow):
        uk = u_ref[k]
        for j in range(npiece):
            lhs_ref[j, pl.ds(k, tc, stride=nrow), :] = uk[:, j * lane:(j + 1) * lane]
    lhs = jnp.concatenate([lhs_ref[j] for j in range(npiece)], axis=1).astype(BF16)
    for h in range(2):
        x_ref[:, h * half_st:(h + 1) * half_st] = _dot(lhs[:, h * half_in:(h + 1) * half_in], wb_ref[h])

    steps = range(tc - 1, -1, -1) if rev else range(tc)
    for h in range(2):
        cr = slice(h * half_st, h * half_st + nc)
        ci = slice(h * half_st + nc, (h + 1) * half_st)
        lr = jnp.broadcast_to(lr_ref[:, h * nc:(h + 1) * nc], (nrow, nc))
        li = jnp.broadcast_to(li_ref[:, h * nc:(h + 1) * nc], (nrow, nc))
        xr = xs_ref[:, cr]
        xi = xs_ref[:, ci]
        for t in steps:
            rows = slice(t * nrow, (t + 1) * nrow)
            nxr = lr * xr - li * xi + x_ref[rows, cr]
            nxi = lr * xi + li * xr + x_ref[rows, ci]
            x_ref[rows, cr] = nxr
            x_ref[rows, ci] = nxi
            xr, xi = nxr, nxi
        xs_ref[:, cr] = xr
        xs_ref[:, ci] = xi

    for h in range(2):
        xh = x_ref[:, h * half_st:(h + 1) * half_st].astype(BF16)
        yh = _dot(xh, wc_ref[h])
        for j in range(npiece // 2):
            ytm_ref[h * (npiece // 2) + j] = yh[:, j * lane:(j + 1) * lane]
    for k in range(nrow):
        y_ref[k] = jnp.concatenate(
            [ytm_ref[j, pl.ds(k, tc, stride=nrow), :] for j in range(npiece)], axis=1)

    if want_state:
        @pl.when(i == nt - 1)
        def _():
            xout_ref[...] = xs_ref[...]


def _s5(z3, wb, wc, lr, li, x0, want_state, rev):
    nb, seq, _ = z3.shape
    bw = wb.shape[2] * 2
    nst = wb.shape[3] * 2
    tc = S5_TC
    nt = seq // tc
    nbg = nb // S5_ROWS
    has_init = x0 is not None
    dirn = int(rev)

    def tix(i):
        return nt - 1 - i if rev else i

    in_specs = [
        pl.BlockSpec((S5_ROWS, tc, bw), lambda g, i: (g, tix(i), 5)),
        pl.BlockSpec((None, 2, bw // 2, nst // 2), lambda g, i: (dirn, 0, 0, 0)),
        pl.BlockSpec((None, 2, nst // 2, bw // 2), lambda g, i: (dirn, 0, 0, 0)),
        pl.BlockSpec((None, 1, nst // 2), lambda g, i: (dirn, 0, 0)),
        pl.BlockSpec((None, 1, nst // 2), lambda g, i: (dirn, 0, 0)),
    ]
    args = [z3, wb, wc, lr, li]
    if has_init:
        in_specs.append(pl.BlockSpec((None, S5_ROWS, nst), lambda g, i: (dirn, g, 0)))
        args.append(x0)
    out_specs = [pl.BlockSpec((S5_ROWS, tc, bw), lambda g, i: (g, tix(i), 0))]
    out_shape = [jax.ShapeDtypeStruct((nb, seq, bw), F32)]
    if want_state:
        out_specs.append(pl.BlockSpec((S5_ROWS, nst), lambda g, i: (g, 0)))
        out_shape.append(jax.ShapeDtypeStruct((nb, nst), F32))
    return pl.pallas_call(
        functools.partial(_s5_body, nt=nt, tc=tc, rev=rev, has_init=has_init, want_state=want_state),
        grid=(nbg, nt),
        in_specs=in_specs,
        out_specs=out_specs,
        out_shape=out_shape,
        scratch_shapes=[
            pltpu.VMEM((bw // LANES, tc * S5_ROWS, LANES), F32),
            pltpu.VMEM((tc * S5_ROWS, nst), F32),
            pltpu.VMEM((S5_ROWS, nst), F32),
            pltpu.VMEM((bw // LANES, tc * S5_ROWS, LANES), F32),
        ],
        compiler_params=pltpu.CompilerParams(dimension_semantics=("arbitrary", "arbitrary")),
        name="s5_bwd" if rev else "s5_fwd",
    )(*args)


def _tail_rows(rows, x_ref, ga_ref, u_ref, gb_ref, of_ref, ob_ref, yf_ref, yb_ref, m0, m1,
               hg_ref, sd_ref, wglu_ref, bglu_ref, wout_ref, ng_ref, win_ref, cw_ref, cb_ref,
               wout2_ref, fg_ref, out_ref, *, d, seg):
    aw = A_HEADS * HEAD
    o = of_ref[rows, :] + ob_ref[rows, :]
    hg = hg_ref[...]
    heads = []
    for h in range(A_HEADS):
        cols = slice(h * HEAD, (h + 1) * HEAD)
        heads.append(_rmsnorm(o[:, cols], hg[:, cols]))
    o_a = jnp.concatenate(heads, axis=1) * ga_ref[rows, :]
    y = yf_ref[rows, :] + yb_ref[rows, :] + sd_ref[...] * u_ref[rows, :]
    y = jax.nn.gelu(y, approximate=True)
    y = y * jax.nn.sigmoid(_dot(y.astype(BF16), wglu_ref[...]) + bglu_ref[...])
    o_b = y * gb_ref[rows, :]
    mixed = _dot(o_a.astype(BF16), wout_ref[0:aw, :]) + _dot(o_b.astype(BF16), wout_ref[aw:, :])
    y1 = x_ref[rows, :] + m0[:, 2 * d:3 * d] * mixed

    hb = (_rmsnorm(y1, ng_ref[...]) * (1.0 + m1[:, d:2 * d]) + m1[:, 0:d]).astype(BF16)
    n = y1.shape[0]
    cblk = 256
    pos = lax.broadcasted_iota(jnp.int32, (n, cblk), 0) % seg
    first = pos == 0
    last = pos == seg - 1
    acc = jnp.zeros((n, d), F32)

    def proj(c):
        return tuple(_dot(hb, win_ref[:, j * d + c * cblk:j * d + (c + 1) * cblk]) for j in range(4))

    ahead = proj(0)
    for c in range(d // cblk):
        cs = slice(c * cblk, (c + 1) * cblk)
        bg, cg, vv, gg = ahead
        if c + 1 < d // cblk:
            ahead = proj(c + 1)
        cv = cg * vv
        prev = jnp.where(first, 0.0, pltpu.roll(cv, 1, 0))
        nxt = jnp.where(last, 0.0, pltpu.roll(cv, n - 1, 0))
        conv = cw_ref[0:1, cs] * prev + cw_ref[1:2, cs] * cv + cw_ref[2:3, cs] * nxt + cb_ref[:, cs]
        y2 = bg * conv * (gg * jax.nn.sigmoid(gg))
        acc = acc + _dot(y2.astype(BF16), wout2_ref[cs, :])
    y2 = y1 + m1[:, 2 * d:3 * d] * acc
    out_ref[rows, :] = _rmsnorm(y2, fg_ref[...])


def _tail_body(x_ref, ga_ref, u_ref, gb_ref, of_ref, ob_ref, yf_ref, yb_ref, m0_ref, m1_ref, *rest, d, seg):
    _tail_rows(slice(None), x_ref, ga_ref, u_ref, gb_ref, of_ref, ob_ref, yf_ref, yb_ref, m0_ref[0], m1_ref[0],
               *rest, d=d, seg=seg)


def _tail(x, z, o_f, o_b, y_f, y_b, mod0, mod1, mod_row, seg, hgrn_g, s5_d, w_glu, b_glu, w_out,
          norm_g, w_in_odd, conv_w, conv_b, w_out_odd, final_g):
    tokens, d = x.shape
    aw = hgrn_g.shape[-1]
    bw = s5_d.shape[-1]

    def const(shape):
        return pl.BlockSpec(shape, lambda i: tuple(0 for _ in shape), pipeline_mode=pl.Buffered(1))

    in_specs = [
        pl.BlockSpec((TILE, d), lambda i: (i, 0)),
        pl.BlockSpec((TILE, aw), lambda i: (i, 4)),
        pl.BlockSpec((TILE, bw), lambda i: (i, 5)),
        pl.BlockSpec((TILE, bw), lambda i: (i, 6)),
        pl.BlockSpec((TILE, aw), lambda i: (i, 0)),
        pl.BlockSpec((TILE, aw), lambda i: (i, 0)),
        pl.BlockSpec((TILE, bw), lambda i: (i, 0)),
        pl.BlockSpec((TILE, bw), lambda i: (i, 0)),
        pl.BlockSpec((1, 1, 3 * d), lambda i: (mod_row(i), 0, 0)),
        pl.BlockSpec((1, 1, 3 * d), lambda i: (mod_row(i), 0, 0)),
        const((1, aw)), const((1, bw)), const(w_glu.shape), const((1, bw)), const(w_out.shape),
        const((1, d)), const(w_in_odd.shape), const(conv_w.shape), const((1, d)),
        const(w_out_odd.shape), const((1, d)),
    ]
    return pl.pallas_call(
        functools.partial(_tail_body, d=d, seg=seg),
        grid=(tokens // TILE,),
        in_specs=in_specs,
        out_specs=pl.BlockSpec((TILE, d), lambda i: (i, 0)),
        out_shape=jax.ShapeDtypeStruct((tokens, d), F32),
        compiler_params=pltpu.CompilerParams(dimension_semantics=("parallel",)),
        name="tail",
    )(x, z, z, z, o_f, o_b, y_f, y_b, mod0, mod1, hgrn_g, s5_d, w_glu, b_glu, w_out,
      norm_g, w_in_odd, conv_w, conv_b, w_out_odd, final_g)


def _block_diag_halves(m):
    ndir, g, a, b = m.shape
    gh = g // 2
    eye = jnp.eye(gh, dtype=m.dtype)
    mh = m.reshape(ndir, 2, gh, a, b)
    out = jnp.einsum('dhgab,gk->dhgakb', mh, eye)
    return out.reshape(ndir, 2, gh * a, gh * b)


def kernel(x_prompt, x_sample, state_hgrn, state_s5_re, state_s5_im, c, c_ctx, norm_g, w_mod, b_mod,
           w_in_even, w_out_even, lb_logits, hgrn_norm_g, s5_lam_re, s5_lam_im, s5_log_dt,
           s5_b_re, s5_b_im, s5_c_re, s5_c_im, s5_d, w_glu, b_glu, w_in_odd, w_out_odd,
           conv_w, conv_b, final_norm_g):
    nbp, seqp, d = x_prompt.shape
    nbs, seqs, _ = x_sample.shape
    depth = norm_g.shape[0]
    n_even, ndir, groups, p = s5_lam_re.shape
    assert depth == 2 and n_even == 1 and ndir == 2, "one even layer followed by one odd layer"
    sg = s5_b_re.shape[-1]
    assert sg == S5_GROUP and lb_logits.shape[-1] == A_HEADS * HEAD
    assert seqp % HGRN_SUB == 0 and seqs % HGRN_BLOCK == 0 and HGRN_BLOCK % HGRN_SUB == 0
    assert TILE % seqp == 0 and seqs % TILE == 0 and (nbp * seqp) % TILE == 0
    assert nbp % S5_ROWS == 0 and nbs == S5_ROWS
    grid_w = 64
    seg_sample = seqs // (seqs // grid_w)

    nstate = ndir * groups * p
    lb_all, lbar_re, lbar_im, bb_re, bb_im = _prepare_params(
        lb_logits,
        s5_lam_re[0].reshape(nstate, 1), s5_lam_im[0].reshape(nstate, 1),
        jnp.broadcast_to(s5_log_dt[0][:, :, None], (ndir, groups, p)).reshape(nstate, 1),
        s5_b_re[0].reshape(nstate, sg), s5_b_im[0].reshape(nstate, sg))
    lb = lb_all[0:1]
    bb_re = bb_re.reshape(ndir, groups, p, sg).transpose(0, 1, 3, 2)
    bb_im = bb_im.reshape(ndir, groups, p, sg).transpose(0, 1, 3, 2)
    wb = jnp.concatenate([_block_diag_halves(bb_re), _block_diag_halves(bb_im)], axis=-1).astype(BF16)
    c_re = s5_c_re[0].transpose(0, 1, 3, 2)
    c_im = s5_c_im[0].transpose(0, 1, 3, 2)
    wc = jnp.concatenate([_block_diag_halves(c_re), _block_diag_halves(-c_im)], axis=-2).astype(BF16)
    lr = lbar_re.reshape(ndir, 1, groups * p)
    li = lbar_im.reshape(ndir, 1, groups * p)

    crow = 16
    cvec = jnp.concatenate([c, c_ctx[None, :], jnp.zeros((crow - nbs - 1, d), F32)], axis=0)
    mod = _adaln(cvec, w_mod, b_mod).reshape(depth, crow, 1, 3 * d)
    tiles_per_seq = seqs // TILE
    row_sample = lambda i: i // tiles_per_seq
    row_prompt = lambda i: nbs

    w_in_b = w_in_even[0].astype(BF16)
    w_out_b = w_out_even[0].astype(BF16)
    w_glu_b = w_glu[0].astype(BF16)
    w_in_odd_b = w_in_odd[0].astype(BF16)
    w_out_odd_b = w_out_odd[0].astype(BF16)

    gh = groups // 2

    def pack_state(re, im):
        nb = re.shape[0]
        re = re.transpose(1, 0, 2, 3).reshape(ndir, nb, 2, gh * p)
        im = im.transpose(1, 0, 2, 3).reshape(ndir, nb, 2, gh * p)
        return jnp.stack([re, im], axis=3).reshape(ndir, nb, 4 * gh * p)

    def unpack_state(xs):
        nb = xs.shape[1]
        xs = xs.reshape(ndir, nb, 2, 2, gh, p)
        re = xs[:, :, :, 0].reshape(ndir, nb, groups, p).transpose(1, 0, 2, 3)
        im = xs[:, :, :, 1].reshape(ndir, nb, groups, p).transpose(1, 0, 2, 3)
        return re, im

    outs = []
    states = None
    for x, nb, seq, mod_row, seg, cached in (
            (x_prompt, nbp, seqp, row_prompt, seqp, False),
            (x_sample, nbs, seqs, row_sample, seg_sample, True)):
        xt = x.reshape(nb * seq, d)
        z = _inproj(xt, mod[0], mod_row, norm_g[0:1], lb, w_in_b)
        z3 = z.reshape(nb, seq, z.shape[-1])
        s0 = state_hgrn[:, 0] if cached else None
        x0 = pack_state(state_s5_re[:, 0], state_s5_im[:, 0]) if cached else None
        hg = [_hgrn(z, nb, seq, s0, not cached, rev) for rev in (False, True)]
        s5 = [_s5(z3, wb, wc, lr, li, x0, not cached, rev) for rev in (False, True)]
        if not cached:
            states = (jnp.stack([hg[0][1], hg[1][1]], axis=1), jnp.stack([s5[0][1], s5[1][1]], axis=0))
        y = _tail(xt, z, hg[0][0], hg[1][0], s5[0][0].reshape(nb * seq, -1), s5[1][0].reshape(nb * seq, -1),
                  mod[0], mod[1], mod_row, seg, hgrn_norm_g[0:1], s5_d[0:1], w_glu_b, b_glu[0:1], w_out_b,
                  norm_g[1:2], w_in_odd_b, conv_w[0], conv_b[0:1], w_out_odd_b, final_norm_g[None, :])
        outs.append(y.reshape(nb, seq, d))

    st_hgrn, st_s5 = states
    new_re, new_im = unpack_state(st_s5)
    return (outs[0], outs[1], st_hgrn[:, None], new_re[:, None], new_im[:, None])
```

```python
import functools

import jax
import jax.numpy as jnp
from jax import lax
from jax.experimental import pallas as pl
from jax.experimental.pallas import tpu as pltpu

F32 = jnp.float32
BF16 = jnp.bfloat16
EPS = 1e-6

A_HEADS = 4
HEAD = 128
S5_GROUP = 16
SCAN_CHUNK = 64
EXP_LIMIT = 80.0
TILE = 512
SUB_TILE = 256
HGRN_BLOCK = 1024
HGRN_SUB = 256
N_IN_BLOCKS = 7
S5_TC = 64
S5_ROWS = 8
LANES = 128
SUBLANES = 8


def _dot(a, b):
    return jnp.dot(a, b, preferred_element_type=F32)


def _dot_nt(a, b):
    return lax.dot_general(a, b, (((1,), (1,)), ((), ())), preferred_element_type=F32)


def _dot_tn(a, b):
    return lax.dot_general(a, b, (((0,), (0,)), ((), ())), preferred_element_type=F32)


def _rmsnorm(x, g):
    return x * lax.rsqrt(jnp.mean(x * x, axis=-1, keepdims=True) + EPS) * g


def _params_body(lbl_ref, lre_ref, lim_ref, ldt_ref, bre_ref, bim_ref,
                 lb_ref, lbr_ref, lbi_ref, bbr_ref, bbi_ref):
    logits = lbl_ref[...]
    e = jnp.exp(logits - jnp.max(logits, axis=0, keepdims=True))
    sm = e / jnp.sum(e, axis=0, keepdims=True)
    acc = sm[0:1]
    lb_ref[0:1, :] = acc
    for r in range(1, logits.shape[0]):
        acc = acc + sm[r:r + 1]
        lb_ref[r:r + 1, :] = acc
    lam_re = lre_ref[...]
    lam_im = lim_ref[...]
    dt = jnp.exp(ldt_ref[...])
    mag = jnp.exp(lam_re * dt)
    lbar_re = mag * jnp.cos(lam_im * dt)
    lbar_im = mag * jnp.sin(lam_im * dt)
    den = lam_re * lam_re + lam_im * lam_im
    coef_re = ((lbar_re - 1.0) * lam_re + lbar_im * lam_im) / den
    coef_im = (lbar_im * lam_re - (lbar_re - 1.0) * lam_im) / den
    lbr_ref[...] = lbar_re
    lbi_ref[...] = lbar_im
    b_re = bre_ref[...]
    b_im = bim_ref[...]
    bbr_ref[...] = coef_re * b_re - coef_im * b_im
    bbi_ref[...] = coef_re * b_im + coef_im * b_re


def _prepare_params(lb_logits, lam_re, lam_im, log_dt, b_re, b_im):
    n, s = b_re.shape
    return pl.pallas_call(
        _params_body,
        out_shape=(
            jax.ShapeDtypeStruct(lb_logits.shape, F32),
            jax.ShapeDtypeStruct((n, 1), F32),
            jax.ShapeDtypeStruct((n, 1), F32),
            jax.ShapeDtypeStruct((n, s), F32),
            jax.ShapeDtypeStruct((n, s), F32),
        ),
        name="prepare_params",
    )(lb_logits, lam_re, lam_im, log_dt, b_re, b_im)


def _adaln_body(c_ref, w_ref, b_ref, m_ref):
    c = c_ref[...]
    s = (c * jax.nn.sigmoid(c)).astype(BF16)
    m_ref[...] = _dot(s, w_ref[...].astype(BF16)) + b_ref[...]


def _adaln(cvec, w_mod, b_mod):
    depth, d, d3 = w_mod.shape
    rows = cvec.shape[0]
    return pl.pallas_call(
        _adaln_body,
        grid=(depth, d3 // d),
        in_specs=[
            pl.BlockSpec((rows, d), lambda l, j: (0, 0)),
            pl.BlockSpec((None, d, d), lambda l, j: (l, 0, j)),
            pl.BlockSpec((None, 1, d), lambda l, j: (l, 0, j)),
        ],
        out_specs=pl.BlockSpec((None, rows, d), lambda l, j: (l, 0, j)),
        out_shape=jax.ShapeDtypeStruct((depth, rows, d3), F32),
        name="adaln",
    )(cvec, w_mod, b_mod.reshape(depth, 1, d3))


def _inproj_body(x_ref, mod_ref, ng_ref, lb_ref, w_ref, z_ref, *, d, aw):
    mod = mod_ref[0]
    lb = lb_ref[...]
    for r in range(x_ref.shape[0] // SUB_TILE):
        rows = slice(r * SUB_TILE, (r + 1) * SUB_TILE)
        h = _rmsnorm(x_ref[rows, :], ng_ref[...]) * (1.0 + mod[:, d:2 * d]) + mod[:, 0:d]
        hb = h.astype(BF16)
        for j in range(N_IN_BLOCKS):
            zj = _dot(hb, w_ref[:, j * aw:(j + 1) * aw])
            if j in (1, 2):
                sg = jax.nn.sigmoid(zj)
                z_ref[rows, (N_IN_BLOCKS - 1 + j) * aw:(N_IN_BLOCKS + j) * aw] = (1.0 - lb) * (1.0 - sg)
                zj = jnp.log(lb + (1.0 - lb) * sg)
            elif j in (4, 6):
                zj = zj * jax.nn.sigmoid(zj)
            z_ref[rows, j * aw:(j + 1) * aw] = zj


def _inproj(x, mod, mod_row, norm_g, lb, w_in):
    tokens, d = x.shape
    aw = lb.shape[-1]
    cols = w_in.shape[1]
    assert cols == N_IN_BLOCKS * aw
    zcols = cols + 2 * aw
    return pl.pallas_call(
        functools.partial(_inproj_body, d=d, aw=aw),
        grid=(tokens // TILE,),
        in_specs=[
            pl.BlockSpec((TILE, d), lambda i: (i, 0)),
            pl.BlockSpec((1, 1, 3 * d), lambda i: (mod_row(i), 0, 0)),
            pl.BlockSpec((1, d), lambda i: (0, 0)),
            pl.BlockSpec((1, aw), lambda i: (0, 0)),
            pl.BlockSpec((d, cols), lambda i: (0, 0), pipeline_mode=pl.Buffered(1)),
        ],
        out_specs=pl.BlockSpec((TILE, zcols), lambda i: (i, 0)),
        out_shape=jax.ShapeDtypeStruct((tokens, zcols), F32),
        compiler_params=pltpu.CompilerParams(dimension_semantics=("parallel",)),
        name="inproj_even",
    )(x, mod, norm_g, lb, w_in)


def _hgrn_body(*refs, nt, nch, nseq, rev, has_init, want_state):
    refs = list(refs)
    q_ref, g_ref, k_ref, v_ref = refs[:4]
    pos = 4
    s0_ref = None
    if has_init:
        s0_ref = refs[pos]
        pos += 1
    o_ref = refs[pos]
    pos += 1
    sout_ref = None
    if want_state:
        sout_ref = refs[pos]
        pos += 1
    st_ref, sold_ref, b_sc = refs[pos:pos + 3]

    i = pl.program_id(1)

    @pl.when(i == 0)
    def _():
        for e in range(nseq):
            for h in range(A_HEADS):
                if has_init:
                    st_ref[e * A_HEADS + h] = s0_ref[e, h].T
                else:
                    st_ref[e * A_HEADS + h] = jnp.zeros((HEAD, HEAD), F32)

    ch = SCAN_CHUNK
    shift = ch.bit_length() - 1
    sub = nch * ch
    nsub = q_ref.shape[0] // sub
    aw = A_HEADS * HEAD
    row = lax.broadcasted_iota(jnp.int32, (sub, sub), 0)
    col = lax.broadcasted_iota(jnp.int32, (sub, sub), 1)
    same_chunk = jnp.right_shift(row, shift) == jnp.right_shift(col, shift)
    mask = jnp.logical_and(same_chunk, (col >= row) if rev else (col <= row))
    cum = jnp.where(mask, 1.0, 0.0).astype(BF16)
    mid = ch // 2
    per_seq = nsub // nseq
    seq_subs = [[e * per_seq + j for j in (range(per_seq - 1, -1, -1) if rev else range(per_seq))]
                for e in range(nseq)]
    order = range(nch - 1, -1, -1) if rev else range(nch)
    heads = [slice(h * HEAD, (h + 1) * HEAD) for h in range(A_HEADS)]
    zero = jnp.zeros((ch, HEAD), BF16)

    def slots(x, h):
        rows = []
        for c in range(nch):
            piece = x[c * ch:(c + 1) * ch, h * HEAD:(h + 1) * HEAD]
            rows.append(jnp.concatenate([zero] * c + [piece] + [zero] * (nch - 1 - c), axis=1))
        return jnp.concatenate(rows, axis=0)

    ends, factors, worst = [], [], None
    for s in range(nsub):
        rs = slice(s * sub, (s + 1) * sub)
        g = g_ref[rs, :]
        e = [jnp.sum(g[c * ch:(c + 1) * ch], axis=0, keepdims=True) for c in range(nch)]
        ends.append(e)
        for x in e:
            worst = x if worst is None else jnp.minimum(worst, x)
        g_hi = g.astype(BF16)
        g_lo = (g - g_hi.astype(F32)).astype(BF16)
        bb = _dot(cum, jnp.concatenate([g_hi, g_lo], axis=1))
        b = bb[:, :aw] + bb[:, aw:]
        b_sc[rs, :] = b
        q = q_ref[rs, :]
        k = k_ref[rs, :]
        qt, kt, qb, kd = [], [], [], []
        for c in range(nch):
            r = slice(c * ch, (c + 1) * ch)
            ref = b[c * ch + mid:c * ch + mid + 1, :]
            grow = jnp.exp(b[r] - ref)
            qg = q[r] * grow
            kg = k[r] / grow
            qt.append(qg.astype(BF16))
            kt.append(kg.astype(BF16))
            qb.append((qg * jnp.exp(ref)).astype(BF16))
            kd.append((kg * jnp.exp(e[c] - ref)).astype(BF16))
        factors.append(tuple(jnp.concatenate(parts, axis=0) for parts in (qt, kt, qb, kd))
                       + (v_ref[rs, :].astype(BF16),))
    safe = jnp.min(worst) > -EXP_LIMIT

    raw = [[_dot_nt(f[0][:, cols], f[1][:, cols]) for cols in heads] for f in factors]
    incr = [[_dot_tn(f[4][:, cols], slots(f[3], h)) for h, cols in enumerate(heads)] for f in factors]
    scores = [[jnp.where(mask, x, 0.0).astype(BF16) for x in per_sub] for per_sub in raw]
    before = [[None] * A_HEADS for _ in range(nsub)]
    for e in range(nseq):
        for h, cols in enumerate(heads):
            st = st_ref[e * A_HEADS + h]
            sold_ref[e * A_HEADS + h] = st
            for s in seq_subs[e]:
                bef = [None] * nch
                for c in order:
                    bef[c] = st.astype(BF16)
                    st = st * jnp.exp(ends[s][c][:, cols]) + incr[s][h][:, c * HEAD:(c + 1) * HEAD]
                before[s][h] = jnp.concatenate(bef, axis=1)
            st_ref[e * A_HEADS + h] = st
    for s, f in enumerate(factors):
        for h, cols in enumerate(heads):
            o_ref[s * sub:(s + 1) * sub, cols] = (
                _dot(scores[s][h], f[4][:, cols]) + _dot_nt(slots(f[2], h), before[s][h]))

    @pl.when(jnp.logical_not(safe))
    def _():
        tpos = lax.broadcasted_iota(jnp.int32, (ch, HEAD), 0)
        for e, h, cols in [(e, h, cols) for e in range(nseq) for h, cols in enumerate(heads)]:
            st = sold_ref[e * A_HEADS + h]
            for s in seq_subs[e]:
                for c in order:
                    r0 = s * sub + c * ch
                    r = slice(r0, r0 + ch)
                    qc = q_ref[r, cols]
                    bc = b_sc[r, cols]
                    end = ends[s][c][:, cols]

                    def body(sg, acc, r0=r0, cols=cols, qc=qc, bc=bc):
                        src = pl.ds(pl.multiple_of(r0 + sg * SUBLANES, SUBLANES), SUBLANES)
                        bs, ks, vs = b_sc[src, cols], k_ref[src, cols], v_ref[src, cols]
                        for j in range(SUBLANES):
                            t = sg * SUBLANES + j
                            valid = (tpos <= t) if rev else (tpos >= t)
                            decay = jnp.exp(jnp.where(valid, bc - bs[j:j + 1], -jnp.inf))
                            a = jnp.sum(qc * decay * ks[j:j + 1], axis=-1, keepdims=True)
                            acc = acc + a * vs[j:j + 1]
                        return acc

                    inter = _dot_nt((qc * jnp.exp(bc)).astype(BF16), st.astype(BF16))
                    o_ref[r, cols] = lax.fori_loop(0, ch // SUBLANES, body, inter)
                    kdc = (k_ref[r, cols] * jnp.exp(end - bc)).astype(BF16)
                    st = st * jnp.exp(end) + _dot_tn(v_ref[r, cols].astype(BF16), kdc)
            st_ref[e * A_HEADS + h] = st

    if want_state:
        @pl.when(i == nt - 1)
        def _():
            for e in range(nseq):
                for h in range(A_HEADS):
                    sout_ref[e, h] = st_ref[e * A_HEADS + h].T


def _hgrn(z, nb, seq, s0, want_state, rev):
    aw = A_HEADS * HEAD
    nseq = max(1, min(HGRN_BLOCK // seq, nb))
    assert nb % nseq == 0
    tb = nseq * seq if nseq > 1 else min(HGRN_BLOCK, seq)
    nt = 1 if nseq > 1 else seq // tb
    nch = HGRN_SUB // SCAN_CHUNK
    has_init = s0 is not None
    dirn = int(rev)

    def blk(b, i):
        return b * nt + (nt - 1 - i if rev else i)

    in_specs = [
        pl.BlockSpec((tb, aw), lambda b, i: (blk(b, i), 0)),
        pl.BlockSpec((tb, aw), lambda b, i: (blk(b, i), 1 + dirn)),
        pl.BlockSpec((tb, aw), lambda b, i: (blk(b, i), N_IN_BLOCKS + dirn)),
        pl.BlockSpec((tb, aw), lambda b, i: (blk(b, i), 3)),
    ]
    args = [z, z, z, z]
    if has_init:
        in_specs.append(pl.BlockSpec((nseq, None, A_HEADS, HEAD, HEAD), lambda b, i: (b, dirn, 0, 0, 0)))
        args.append(s0)
    out_specs = [pl.BlockSpec((tb, aw), lambda b, i: (blk(b, i), 0))]
    out_shape = [jax.ShapeDtypeStruct((nb * seq, aw), F32)]
    if want_state:
        out_specs.append(pl.BlockSpec((nseq, A_HEADS, HEAD, HEAD), lambda b, i: (b, 0, 0, 0)))
        out_shape.append(jax.ShapeDtypeStruct((nb, A_HEADS, HEAD, HEAD), F32))
    return pl.pallas_call(
        functools.partial(_hgrn_body, nt=nt, nch=nch, nseq=nseq, rev=rev, has_init=has_init,
                          want_state=want_state),
        grid=(nb // nseq, nt),
        in_specs=in_specs,
        out_specs=out_specs,
        out_shape=out_shape,
        scratch_shapes=[
            pltpu.VMEM((nseq * A_HEADS, HEAD, HEAD), F32),
            pltpu.VMEM((nseq * A_HEADS, HEAD, HEAD), F32),
            pltpu.VMEM((tb, aw), F32),
        ],
        compiler_params=pltpu.CompilerParams(dimension_semantics=("arbitrary", "arbitrary")),
        name="hgrn_bwd" if rev else "hgrn_fwd",
    )(*args)


def _s5_body(*refs, nt, tc, rev, has_init, want_state):
    refs = list(refs)
    u_ref, wb_ref, wc_ref, lr_ref, li_ref = refs[:5]
    pos = 5
    x0_ref = None
    if has_init:
        x0_ref = refs[pos]
        pos += 1
    y_ref = refs[pos]
    pos += 1
    xout_ref = None
    if want_state:
        xout_ref = refs[pos]
        pos += 1
    lhs_ref, x_ref, xs_ref, ytm_ref = refs[pos:pos + 4]

    i = pl.program_id(1)
    nrow = S5_ROWS
    half_in = wb_ref.shape[1]
    half_st = x_ref.shape[1] // 2
    nc = half_st // 2

    @pl.when(i == 0)
    def _():
        if has_init:
            xs_ref[...] = x0_ref[...]
        else:
            xs_ref[...] = jnp.zeros(xs_ref.shape, F32)

    lane = lhs_ref.shape[2]
    npiece = lhs_ref.shape[0]
    for k in range(nrow):
        uk = u_ref[k]
        for j in range(npiece):
            lhs_ref[j, pl.ds(k, tc, stride=nrow), :] = uk[:, j * lane:(j + 1) * lane]
    lhs = jnp.concatenate([lhs_ref[j] for j in range(npiece)], axis=1).astype(BF16)
    for h in range(2):
        x_ref[:, h * half_st:(h + 1) * half_st] = _dot(lhs[:, h * half_in:(h + 1) * half_in], wb_ref[h])

    steps = range(tc - 1, -1, -1) if rev else range(tc)
    for h in range(2):
        cr = slice(h * half_st, h * half_st + nc)
        ci = slice(h * half_st + nc, (h + 1) * half_st)
        lr = jnp.broadcast_to(lr_ref[:, h * nc:(h + 1) * nc], (nrow, nc))
        li = jnp.broadcast_to(li_ref[:, h * nc:(h + 1) * nc], (nrow, nc))
        xr = xs_ref[:, cr]
        xi = xs_ref[:, ci]
        for t in steps:
            rows = slice(t * nrow, (t + 1) * nrow)
            nxr = lr * xr - li * xi + x_ref[rows, cr]
            nxi = lr * xi + li * xr + x_ref[rows, ci]
            x_ref[rows, cr] = nxr
            x_ref[rows, ci] = nxi
            xr, xi = nxr, nxi
        xs_ref[:, cr] = xr
        xs_ref[:, ci] = xi

    for h in range(2):
        xh = x_ref[:, h * half_st:(h + 1) * half_st].astype(BF16)
        yh = _dot(xh, wc_ref[h])
        for j in range(npiece // 2):
            ytm_ref[h * (npiece // 2) + j] = yh[:, j * lane:(j + 1) * lane]
    for k in range(nrow):
        y_ref[k] = jnp.concatenate(
            [ytm_ref[j, pl.ds(k, tc, stride=nrow), :] for j in range(npiece)], axis=1)

    if want_state:
        @pl.when(i == nt - 1)
        def _():
            xout_ref[...] = xs_ref[...]


def _s5(z3, wb, wc, lr, li, x0, want_state, rev):
    nb, seq, _ = z3.shape
    bw = wb.shape[2] * 2
    nst = wb.shape[3] * 2
    tc = S5_TC
    nt = seq // tc
    nbg = nb // S5_ROWS
    has_init = x0 is not None
    dirn = int(rev)

    def tix(i):
        return nt - 1 - i if rev else i

    in_specs = [
        pl.BlockSpec((S5_ROWS, tc, bw), lambda g, i: (g, tix(i), 5)),
        pl.BlockSpec((None, 2, bw // 2, nst // 2), lambda g, i: (dirn, 0, 0, 0)),
        pl.BlockSpec((None, 2, nst // 2, bw // 2), lambda g, i: (dirn, 0, 0, 0)),
        pl.BlockSpec((None, 1, nst // 2), lambda g, i: (dirn, 0, 0)),
        pl.BlockSpec((None, 1, nst // 2), lambda g, i: (dirn, 0, 0)),
    ]
    args = [z3, wb, wc, lr, li]
    if has_init:
        in_specs.append(pl.BlockSpec((None, S5_ROWS, nst), lambda g, i: (dirn, g, 0)))
        args.append(x0)
    out_specs = [pl.BlockSpec((S5_ROWS, tc, bw), lambda g, i: (g, tix(i), 0))]
    out_shape = [jax.ShapeDtypeStruct((nb, seq, bw), F32)]
    if want_state:
        out_specs.append(pl.BlockSpec((S5_ROWS, nst), lambda g, i: (g, 0)))
        out_shape.append(jax.ShapeDtypeStruct((nb, nst), F32))
    return pl.pallas_call(
        functools.partial(_s5_body, nt=nt, tc=tc, rev=rev, has_init=has_init, want_state=want_state),
        grid=(nbg, nt),
        in_specs=in_specs,
        out_specs=out_specs,
        out_shape=out_shape,
        scratch_shapes=[
            pltpu.VMEM((bw // LANES, tc * S5_ROWS, LANES), F32),
            pltpu.VMEM((tc * S5_ROWS, nst), F32),
            pltpu.VMEM((S5_ROWS, nst), F32),
            pltpu.VMEM((bw // LANES, tc * S5_ROWS, LANES), F32),
        ],
        compiler_params=pltpu.CompilerParams(dimension_semantics=("arbitrary", "arbitrary")),
        name="s5_bwd" if rev else "s5_fwd",
    )(*args)


def _s5_pair_body(*refs, nt, tc, has_init, want_state):
    refs = list(refs)
    u_refs = refs[0:2]
    wb_ref, wc_ref, lr_ref, li_ref = refs[2:6]
    pos = 6
    x0_ref = None
    if has_init:
        x0_ref = refs[pos]
        pos += 1
    y_refs = refs[pos:pos + 2]
    pos += 2
    xout_ref = None
    if want_state:
        xout_ref = refs[pos]
        pos += 1
    lhs_refs = refs[pos:pos + 2]
    x_refs = refs[pos + 2:pos + 4]
    ytm_refs = refs[pos + 4:pos + 6]
    xs_ref = refs[pos + 6]

    i = pl.program_id(1)
    nrow = S5_ROWS
    half_in = wb_ref.shape[2]
    half_st = x_refs[0].shape[1] // 2
    nc = half_st // 2
    lane = lhs_refs[0].shape[2]
    npiece = lhs_refs[0].shape[0]

    @pl.when(i == 0)
    def _():
        if has_init:
            xs_ref[...] = x0_ref[...]
        else:
            xs_ref[...] = jnp.zeros(xs_ref.shape, F32)

    def in_proj(d):
        for k in range(nrow):
            uk = u_refs[d][k]
            for j in range(npiece):
                lhs_refs[d][j, pl.ds(k, tc, stride=nrow), :] = uk[:, j * lane:(j + 1) * lane]
        lhs = jnp.concatenate([lhs_refs[d][j] for j in range(npiece)], axis=1).astype(BF16)
        for h in range(2):
            x_refs[d][:, h * half_st:(h + 1) * half_st] = _dot(
                lhs[:, h * half_in:(h + 1) * half_in], wb_ref[d, h])

    def scan(d):
        x_ref = x_refs[d]
        steps = range(tc - 1, -1, -1) if d else range(tc)
        for h in range(2):
            cr = slice(h * half_st, h * half_st + nc)
            ci = slice(h * half_st + nc, (h + 1) * half_st)
            lr = jnp.broadcast_to(lr_ref[d, :, h * nc:(h + 1) * nc], (nrow, nc))
            li = jnp.broadcast_to(li_ref[d, :, h * nc:(h + 1) * nc], (nrow, nc))
            xr = xs_ref[d, :, cr]
            xi = xs_ref[d, :, ci]
            for t in steps:
                rows = slice(t * nrow, (t + 1) * nrow)
                nxr = lr * xr - li * xi + x_ref[rows, cr]
                nxi = lr * xi + li * xr + x_ref[rows, ci]
                x_ref[rows, cr] = nxr
                x_ref[rows, ci] = nxi
                xr, xi = nxr, nxi
            xs_ref[d, :, cr] = xr
            xs_ref[d, :, ci] = xi

    def out_proj(d):
        for h in range(2):
            xh = x_refs[d][:, h * half_st:(h + 1) * half_st].astype(BF16)
            yh = _dot(xh, wc_ref[d, h])
            for j in range(npiece // 2):
                ytm_refs[d][h * (npiece // 2) + j] = yh[:, j * lane:(j + 1) * lane]
        for k in range(nrow):
            y_refs[d][k] = jnp.concatenate(
                [ytm_refs[d][j, pl.ds(k, tc, stride=nrow), :] for j in range(npiece)], axis=1)

    in_proj(0)
    in_proj(1)
    scan(0)
    out_proj(0)
    scan(1)
    out_proj(1)

    if want_state:
        @pl.when(i == nt - 1)
        def _():
            xout_ref[...] = xs_ref[...]


def _s5_pair(z3, wb, wc, lr, li, x0, want_state):
    nb, seq, _ = z3.shape
    bw = wb.shape[2] * 2
    nst = wb.shape[3] * 2
    tc = S5_TC
    nt = seq // tc
    nbg = nb // S5_ROWS
    has_init = x0 is not None

    def whole(a):
        return pl.BlockSpec(a.shape, lambda g, i: tuple(0 for _ in a.shape))

    in_specs = [
        pl.BlockSpec((S5_ROWS, tc, bw), lambda g, i: (g, i, 5)),
        pl.BlockSpec((S5_ROWS, tc, bw), lambda g, i: (g, nt - 1 - i, 5)),
        whole(wb), whole(wc), whole(lr), whole(li),
    ]
    args = [z3, z3, wb, wc, lr, li]
    if has_init:
        in_specs.append(pl.BlockSpec((2, S5_ROWS, nst), lambda g, i: (0, g, 0)))
        args.append(x0)
    out_specs = [pl.BlockSpec((S5_ROWS, tc, bw), lambda g, i: (g, i, 0)),
                 pl.BlockSpec((S5_ROWS, tc, bw), lambda g, i: (g, nt - 1 - i, 0))]
    out_shape = [jax.ShapeDtypeStruct((nb, seq, bw), F32)] * 2
    if want_state:
        out_specs.append(pl.BlockSpec((2, S5_ROWS, nst), lambda g, i: (0, g, 0)))
        out_shape.append(jax.ShapeDtypeStruct((2, nb, nst), F32))
    rows = tc * S5_ROWS
    return pl.pallas_call(
        functools.partial(_s5_pair_body, nt=nt, tc=tc, has_init=has_init, want_state=want_state),
        grid=(nbg, nt),
        in_specs=in_specs,
        out_specs=out_specs,
        out_shape=out_shape,
        scratch_shapes=(
            [pltpu.VMEM((bw // LANES, rows, LANES), F32)] * 2
            + [pltpu.VMEM((rows, nst), F32)] * 2
            + [pltpu.VMEM((bw // LANES, rows, LANES), F32)] * 2
            + [pltpu.VMEM((2, S5_ROWS, nst), F32)]),
        compiler_params=pltpu.CompilerParams(dimension_semantics=("arbitrary", "arbitrary")),
        name="s5_pair",
    )(*args)


def _tail_rows(rows, x_ref, ga_ref, u_ref, gb_ref, of_ref, ob_ref, yf_ref, yb_ref, m0, m1,
               hg_ref, sd_ref, wglu_ref, bglu_ref, wout_ref, ng_ref, win_ref, cw_ref, cb_ref,
               wout2_ref, fg_ref, out_ref, *, d, seg):
    aw = A_HEADS * HEAD
    o = of_ref[rows, :] + ob_ref[rows, :]
    hg = hg_ref[...]
    heads = []
    for h in range(A_HEADS):
        cols = slice(h * HEAD, (h + 1) * HEAD)
        heads.append(_rmsnorm(o[:, cols], hg[:, cols]))
    o_a = jnp.concatenate(heads, axis=1) * ga_ref[rows, :]
    y = yf_ref[rows, :] + yb_ref[rows, :] + sd_ref[...] * u_ref[rows, :]
    y = jax.nn.gelu(y, approximate=True)
    y = y * jax.nn.sigmoid(_dot(y.astype(BF16), wglu_ref[...]) + bglu_ref[...])
    o_b = y * gb_ref[rows, :]
    mixed = _dot(o_a.astype(BF16), wout_ref[0:aw, :]) + _dot(o_b.astype(BF16), wout_ref[aw:, :])
    y1 = x_ref[rows, :] + m0[:, 2 * d:3 * d] * mixed

    hb = (_rmsnorm(y1, ng_ref[...]) * (1.0 + m1[:, d:2 * d]) + m1[:, 0:d]).astype(BF16)
    n = y1.shape[0]
    cblk = 256
    pos = lax.broadcasted_iota(jnp.int32, (n, cblk), 0) % seg
    first = pos == 0
    last = pos == seg - 1
    acc = jnp.zeros((n, d), F32)

    def proj(c):
        return tuple(_dot(hb, win_ref[:, j * d + c * cblk:j * d + (c + 1) * cblk]) for j in range(4))

    ahead = proj(0)
    for c in range(d // cblk):
        cs = slice(c * cblk, (c + 1) * cblk)
        bg, cg, vv, gg = ahead
        if c + 1 < d // cblk:
            ahead = proj(c + 1)
        cv = cg * vv
        prev = jnp.where(first, 0.0, pltpu.roll(cv, 1, 0))
        nxt = jnp.where(last, 0.0, pltpu.roll(cv, n - 1, 0))
        conv = cw_ref[0:1, cs] * prev + cw_ref[1:2, cs] * cv + cw_ref[2:3, cs] * nxt + cb_ref[:, cs]
        y2 = bg * conv * (gg * jax.nn.sigmoid(gg))
        acc = acc + _dot(y2.astype(BF16), wout2_ref[cs, :])
    y2 = y1 + m1[:, 2 * d:3 * d] * acc
    out_ref[rows, :] = _rmsnorm(y2, fg_ref[...])


def _tail_body(x_ref, ga_ref, u_ref, gb_ref, of_ref, ob_ref, yf_ref, yb_ref, m0_ref, m1_ref, *rest, d, seg):
    _tail_rows(slice(None), x_ref, ga_ref, u_ref, gb_ref, of_ref, ob_ref, yf_ref, yb_ref, m0_ref[0], m1_ref[0],
               *rest, d=d, seg=seg)


def _tail(x, z, o_f, o_b, y_f, y_b, mod0, mod1, mod_row, seg, hgrn_g, s5_d, w_glu, b_glu, w_out,
          norm_g, w_in_odd, conv_w, conv_b, w_out_odd, final_g):
    tokens, d = x.shape
    aw = hgrn_g.shape[-1]
    bw = s5_d.shape[-1]

    def const(shape):
        return pl.BlockSpec(shape, lambda i: tuple(0 for _ in shape), pipeline_mode=pl.Buffered(1))

    in_specs = [
        pl.BlockSpec((TILE, d), lambda i: (i, 0)),
        pl.BlockSpec((TILE, aw), lambda i: (i, 4)),
        pl.BlockSpec((TILE, bw), lambda i: (i, 5)),
        pl.BlockSpec((TILE, bw), lambda i: (i, 6)),
        pl.BlockSpec((TILE, aw), lambda i: (i, 0)),
        pl.BlockSpec((TILE, aw), lambda i: (i, 0)),
        pl.BlockSpec((TILE, bw), lambda i: (i, 0)),
        pl.BlockSpec((TILE, bw), lambda i: (i, 0)),
        pl.BlockSpec((1, 1, 3 * d), lambda i: (mod_row(i), 0, 0)),
        pl.BlockSpec((1, 1, 3 * d), lambda i: (mod_row(i), 0, 0)),
        const((1, aw)), const((1, bw)), const(w_glu.shape), const((1, bw)), const(w_out.shape),
        const((1, d)), const(w_in_odd.shape), const(conv_w.shape), const((1, d)),
        const(w_out_odd.shape), const((1, d)),
    ]
    return pl.pallas_call(
        functools.partial(_tail_body, d=d, seg=seg),
        grid=(tokens // TILE,),
        in_specs=in_specs,
        out_specs=pl.BlockSpec((TILE, d), lambda i: (i, 0)),
        out_shape=jax.ShapeDtypeStruct((tokens, d), F32),
        compiler_params=pltpu.CompilerParams(dimension_semantics=("parallel",)),
        name="tail",
    )(x, z, z, z, o_f, o_b, y_f, y_b, mod0, mod1, hgrn_g, s5_d, w_glu, b_glu, w_out,
      norm_g, w_in_odd, conv_w, conv_b, w_out_odd, final_g)


def _block_diag_halves(m):
    ndir, g, a, b = m.shape
    gh = g // 2
    eye = jnp.eye(gh, dtype=m.dtype)
    mh = m.reshape(ndir, 2, gh, a, b)
    out = jnp.einsum('dhgab,gk->dhgakb', mh, eye)
    return out.reshape(ndir, 2, gh * a, gh * b)


def kernel(x_prompt, x_sample, state_hgrn, state_s5_re, state_s5_im, c, c_ctx, norm_g, w_mod, b_mod,
           w_in_even, w_out_even, lb_logits, hgrn_norm_g, s5_lam_re, s5_lam_im, s5_log_dt,
           s5_b_re, s5_b_im, s5_c_re, s5_c_im, s5_d, w_glu, b_glu, w_in_odd, w_out_odd,
           conv_w, conv_b, final_norm_g):
    nbp, seqp, d = x_prompt.shape
    nbs, seqs, _ = x_sample.shape
    depth = norm_g.shape[0]
    n_even, ndir, groups, p = s5_lam_re.shape
    assert depth == 2 and n_even == 1 and ndir == 2, "one even layer followed by one odd layer"
    sg = s5_b_re.shape[-1]
    assert sg == S5_GROUP and lb_logits.shape[-1] == A_HEADS * HEAD
    assert seqp % HGRN_SUB == 0 and seqs % HGRN_BLOCK == 0 and HGRN_BLOCK % HGRN_SUB == 0
    assert TILE % seqp == 0 and seqs % TILE == 0 and (nbp * seqp) % TILE == 0
    assert nbp % S5_ROWS == 0 and nbs == S5_ROWS
    grid_w = 64
    seg_sample = seqs // (seqs // grid_w)

    nstate = ndir * groups * p
    lb_all, lbar_re, lbar_im, bb_re, bb_im = _prepare_params(
        lb_logits,
        s5_lam_re[0].reshape(nstate, 1), s5_lam_im[0].reshape(nstate, 1),
        jnp.broadcast_to(s5_log_dt[0][:, :, None], (ndir, groups, p)).reshape(nstate, 1),
        s5_b_re[0].reshape(nstate, sg), s5_b_im[0].reshape(nstate, sg))
    lb = lb_all[0:1]
    bb_re = bb_re.reshape(ndir, groups, p, sg).transpose(0, 1, 3, 2)
    bb_im = bb_im.reshape(ndir, groups, p, sg).transpose(0, 1, 3, 2)
    wb = jnp.concatenate([_block_diag_halves(bb_re), _block_diag_halves(bb_im)], axis=-1).astype(BF16)
    c_re = s5_c_re[0].transpose(0, 1, 3, 2)
    c_im = s5_c_im[0].transpose(0, 1, 3, 2)
    wc = jnp.concatenate([_block_diag_halves(c_re), _block_diag_halves(-c_im)], axis=-2).astype(BF16)
    lr = lbar_re.reshape(ndir, 1, groups * p)
    li = lbar_im.reshape(ndir, 1, groups * p)

    crow = 16
    cvec = jnp.concatenate([c, c_ctx[None, :], jnp.zeros((crow - nbs - 1, d), F32)], axis=0)
    mod = _adaln(cvec, w_mod, b_mod).reshape(depth, crow, 1, 3 * d)
    tiles_per_seq = seqs // TILE
    row_sample = lambda i: i // tiles_per_seq
    row_prompt = lambda i: nbs

    w_in_b = w_in_even[0].astype(BF16)
    w_out_b = w_out_even[0].astype(BF16)
    w_glu_b = w_glu[0].astype(BF16)
    w_in_odd_b = w_in_odd[0].astype(BF16)
    w_out_odd_b = w_out_odd[0].astype(BF16)

    gh = groups // 2

    def pack_state(re, im):
        nb = re.shape[0]
        re = re.transpose(1, 0, 2, 3).reshape(ndir, nb, 2, gh * p)
        im = im.transpose(1, 0, 2, 3).reshape(ndir, nb, 2, gh * p)
        return jnp.stack([re, im], axis=3).reshape(ndir, nb, 4 * gh * p)

    def unpack_state(xs):
        nb = xs.shape[1]
        xs = xs.reshape(ndir, nb, 2, 2, gh, p)
        re = xs[:, :, :, 0].reshape(ndir, nb, groups, p).transpose(1, 0, 2, 3)
        im = xs[:, :, :, 1].reshape(ndir, nb, groups, p).transpose(1, 0, 2, 3)
        return re, im

    outs = []
    states = None
    for x, nb, seq, mod_row, seg, cached in (
            (x_prompt, nbp, seqp, row_prompt, seqp, False),
            (x_sample, nbs, seqs, row_sample, seg_sample, True)):
        xt = x.reshape(nb * seq, d)
        z = _inproj(xt, mod[0], mod_row, norm_g[0:1], lb, w_in_b)
        z3 = z.reshape(nb, seq, z.shape[-1])
        s0 = state_hgrn[:, 0] if cached else None
        x0 = pack_state(state_s5_re[:, 0], state_s5_im[:, 0]) if cached else None
        hg = [_hgrn(z, nb, seq, s0, not cached, rev) for rev in (False, True)]
        s5 = _s5_pair(z3, wb, wc, lr, li, x0, not cached)
        if not cached:
            states = (jnp.stack([hg[0][1], hg[1][1]], axis=1), s5[2])
        y = _tail(xt, z, hg[0][0], hg[1][0], s5[0].reshape(nb * seq, -1), s5[1].reshape(nb * seq, -1),
                  mod[0], mod[1], mod_row, seg, hgrn_norm_g[0:1], s5_d[0:1], w_glu_b, b_glu[0:1], w_out_b,
                  norm_g[1:2], w_in_odd_b, conv_w[0], conv_b[0:1], w_out_odd_b, final_norm_g[None, :])
        outs.append(y.reshape(nb, seq, d))

    st_hgrn, st_s5 = states
    new_re, new_im = unpack_state(st_s5)
    return (outs[0], outs[1], st_hgrn[:, None], new_re[:, None], new_im[:, None])
```
